```python
import math, functools
import jax, jax.numpy as jnp
from jax import lax
import numpy as np

D_MODEL = 2048
BATCH = 32
SEQ = 256
DEPTH = 4
DEC_BATCH = 2
DEC_SEQ = 1024
PAST_LEN = 512

GRID_W = 64
HEAD_DIM = 128
N_HEADS_A = 8
W_A = N_HEADS_A * HEAD_DIM
GROUP_DIM = 128
N_GROUPS_F = 4
W_F = N_GROUPS_F * GROUP_DIM
N_GROUPS_C = 4
W_C = N_GROUPS_C * GROUP_DIM
CHUNK = 128
MAX_WIN_R = 8
WIN_C = 16
QBLK = 128
D_FF = 4 * D_MODEL
N_MOD = 6
IN_WIDTH = 3 * W_A + W_F + 2 * W_C + 3 * D_MODEL
RMS_EPS = 1e-6

kernel_name = "hybrid_natten_fnet_gmlp_diffusion_step"


def _rmsnorm(x, g):
    x32 = x.astype(jnp.float32)
    y = x32 * lax.rsqrt(jnp.mean(x32 * x32, axis=-1, keepdims=True) + RMS_EPS)
    return (y * g.astype(jnp.float32)).astype(x.dtype)


def _context_attention(q, k, v):
    B, S, H, hd = q.shape
    scale = 1.0 / math.sqrt(hd)
    qb = q.reshape(B, S // QBLK, QBLK, H, hd).transpose(1, 0, 2, 3, 4)

    def one_block(qi):
        s = jnp.einsum('bqhd,bkhd->bhqk', qi, k).astype(jnp.float32) * scale
        p = jax.nn.softmax(s, axis=-1).astype(v.dtype)
        return jnp.einsum('bhqk,bkhd->bqhd', p, v)

    o = lax.map(one_block, qb)
    return o.transpose(1, 0, 2, 3, 4).reshape(B, S, H * hd)


def _neighbourhood_attention(q, k, v, ck, cv, rpb):
    B, N, H, hd = q.shape
    rows = N // GRID_W
    win_r = min(MAX_WIN_R, rows)
    scale = 1.0 / math.sqrt(hd)
    r = jnp.arange(rows)
    r_start = jnp.clip(r - win_r // 2, 0, rows - win_r)
    row_idx = r_start[:, None] + jnp.arange(win_r)[None, :]
    qc = jnp.arange(GRID_W)
    c_start = jnp.clip(qc - WIN_C // 2, 0, GRID_W - WIN_C)
    kc = jnp.arange(GRID_W)
    col_valid = (kc[None, :] >= c_start[:, None]) & (kc[None, :] < c_start[:, None] + WIN_C)
    dr = row_idx - r[:, None] + (MAX_WIN_R - 1)
    dc = jnp.clip(kc[None, :] - qc[:, None], -(WIN_C - 1), WIN_C - 1) + (WIN_C - 1)
    bias = rpb[:, dr[:, None, :, None], dc[None, :, None, :]]
    bias = bias.transpose(1, 2, 0, 3, 4).astype(jnp.float32)

    qg = q.reshape(B, rows, GRID_W, H, hd)
    kg = k.reshape(B, rows, GRID_W, H, hd)
    vg = v.reshape(B, rows, GRID_W, H, hd)
    k_rows = kg[:, row_idx]
    v_rows = vg[:, row_idx]

    s_lat = jnp.einsum('brqhd,brikhd->brqhik', qg, k_rows).astype(jnp.float32) * scale + bias[None]
    s_lat = jnp.where(col_valid[None, None, :, None, None, :], s_lat, -jnp.inf)
    s_ctx = jnp.einsum('brqhd,blhd->brqhl', qg, ck).astype(jnp.float32) * scale
    n_lat = win_r * GRID_W
    s = jnp.concatenate([s_lat.reshape(B, rows, GRID_W, H, n_lat), s_ctx], axis=-1)
    p = jax.nn.softmax(s, axis=-1).astype(v.dtype)
    p_lat = p[..., :n_lat].reshape(B, rows, GRID_W, H, win_r, GRID_W)
    p_ctx = p[..., n_lat:]
    o = (jnp.einsum('brqhik,brikhd->brqhd', p_lat, v_rows)
         + jnp.einsum('brqhl,blhd->brqhd', p_ctx, cv))
    return o.reshape(B, N, H * hd)


def _fourier_mix(f):
    B, N, _ = f.shape
    fg = f.reshape(B, N, N_GROUPS_F, GROUP_DIM).astype(jnp.float32)
    y = jnp.fft.fft2(fg, axes=(1, 3), norm='ortho').real
    return y.astype(f.dtype).reshape(B, N, W_F)


def _spatial_gating(uv, g_sgu, w_sp, b_sp):
    B, N, _ = uv.shape
    uv = jax.nn.gelu(uv, approximate=False)
    u, v = jnp.split(uv, 2, axis=-1)
    v = _rmsnorm(v.reshape(B, N, N_GROUPS_C, GROUP_DIM), g_sgu)
    vc = v.reshape(B, N // CHUNK, CHUNK, N_GROUPS_C, GROUP_DIM)
    s = jnp.einsum('gpq,bcqgd->bcpgd', w_sp, vc) + b_sp.T[:, :, None]
    return u * s.reshape(B, N, W_C)


def _layer(x, mod, attend, w_in, w_br_a, w_br_f, w_br_c, w_out, g_sgu, w_sp, b_sp,
           g_pre_mix, g_post_mix, g_pre_mlp, g_post_mlp, w1, w2):
    sh1, sc1, gt1, sh2, sc2, gt2 = jnp.split(mod, N_MOD, axis=-1)
    B, N, _ = x.shape
    h = _rmsnorm(x, g_pre_mix) * (1.0 + sc1) + sh1
    proj = h @ w_in
    splits = (W_A, 2 * W_A, 3 * W_A, 3 * W_A + W_F, 3 * W_A + W_F + 2 * W_C,
              3 * W_A + W_F + 2 * W_C + D_MODEL, 3 * W_A + W_F + 2 * W_C + 2 * D_MODEL)
    q, k, v, f, uv, ga, gf, gc = jnp.split(proj, splits, axis=-1)
    q = q.reshape(B, N, N_HEADS_A, HEAD_DIM)
    k = k.reshape(B, N, N_HEADS_A, HEAD_DIM)
    v = v.reshape(B, N, N_HEADS_A, HEAD_DIM)
    o_a = attend(q, k, v)
    o_f = _fourier_mix(f)
    o_c = _spatial_gating(uv, g_sgu, w_sp, b_sp)
    merged = (jax.nn.sigmoid(ga) * (o_a @ w_br_a)
              + jax.nn.sigmoid(gf) * (o_f @ w_br_f)
              + jax.nn.sigmoid(gc) * (o_c @ w_br_c))
    x = x + gt1 * _rmsnorm(merged @ w_out, g_post_mix)
    h = _rmsnorm(x, g_pre_mlp) * (1.0 + sc2) + sh2
    ff = jnp.square(jax.nn.relu(h @ w1)) @ w2
    x = x + gt2 * _rmsnorm(ff, g_post_mlp)
    return x, k, v


def setup_inputs(seed: int = 0) -> dict:
    key = jax.random.key(seed)
    ks = jax.random.split(key, 32)

    def nrm(k, shape, scale):
        return jax.random.normal(k, shape, jnp.float32) * scale

    def gain(k, shape):
        return 1.0 + 0.02 * jax.random.normal(k, shape, jnp.float32)

    return {
        "x_prompt": nrm(ks[0], (BATCH, SEQ, D_MODEL), 1.0),
        "x_sample": nrm(ks[1], (DEC_BATCH, DEC_SEQ, D_MODEL), 1.0),
        "cache_k": nrm(ks[2], (DEC_BATCH, DEPTH, PAST_LEN, N_HEADS_A, HEAD_DIM), 1.0),
        "cache_v": nrm(ks[3], (DEC_BATCH, DEPTH, PAST_LEN, N_HEADS_A, HEAD_DIM), 1.0),
        "c": nrm(ks[4], (DEC_BATCH, D_MODEL), 1.0),
        "c_ctx": nrm(ks[5], (D_MODEL,), 1.0),
        "w_ada": nrm(ks[6], (DEPTH, D_MODEL, N_MOD * D_MODEL), 0.5 * D_MODEL ** -0.5),
        "b_ada": nrm(ks[7], (DEPTH, N_MOD * D_MODEL), 0.02),
        "g_pre_mix": gain(ks[8], (DEPTH, D_MODEL)),
        "g_post_mix": gain(ks[9], (DEPTH, D_MODEL)),
        "g_pre_mlp": gain(ks[10], (DEPTH, D_MODEL)),
        "g_post_mlp": gain(ks[11], (DEPTH, D_MODEL)),
        "w_in": nrm(ks[12], (DEPTH, D_MODEL, IN_WIDTH), D_MODEL ** -0.5),
        "rpb": nrm(ks[13], (DEPTH, N_HEADS_A, 2 * MAX_WIN_R - 1, 2 * WIN_C - 1), 0.5),
        "g_sgu": gain(ks[14], (DEPTH, N_GROUPS_C, GROUP_DIM)),
        "w_spatial": nrm(ks[15], (DEPTH, N_GROUPS_C, CHUNK, CHUNK), CHUNK ** -0.5),
        "b_spatial": gain(ks[16], (DEPTH, N_GROUPS_C, CHUNK)),
        "w_br_a": nrm(ks[17], (DEPTH, W_A, D_MODEL), W_A ** -0.5),
        "w_br_f": nrm(ks[18], (DEPTH, W_F, D_MODEL), W_F ** -0.5),
        "w_br_c": nrm(ks[19], (DEPTH, W_C, D_MODEL), W_C ** -0.5),
        "w_out": nrm(ks[20], (DEPTH, D_MODEL, D_MODEL), D_MODEL ** -0.5),
        "w_mlp1": nrm(ks[21], (DEPTH, D_MODEL, D_FF), D_MODEL ** -0.5),
        "w_mlp2": nrm(ks[22], (DEPTH, D_FF, D_MODEL), D_FF ** -0.5),
    }


def reference(x_prompt, x_sample, cache_k, cache_v, c, c_ctx, w_ada, b_ada,
              g_pre_mix, g_post_mix, g_pre_mlp, g_post_mlp, w_in, rpb, g_sgu,
              w_spatial, b_spatial, w_br_a, w_br_f, w_br_c, w_out, w_mlp1, w_mlp2):
    y_p = x_prompt
    y_s = x_sample
    silu_ctx = jax.nn.silu(c_ctx)
    silu_c = jax.nn.silu(c)
    new_k, new_v = [], []
    for l in range(DEPTH):
        shared = (w_in[l], w_br_a[l], w_br_f[l], w_br_c[l], w_out[l], g_sgu[l],
                  w_spatial[l], b_spatial[l], g_pre_mix[l], g_post_mix[l],
                  g_pre_mlp[l], g_post_mlp[l], w_mlp1[l], w_mlp2[l])
        mod_ctx = silu_ctx @ w_ada[l] + b_ada[l]
        mod_lat = (silu_c @ w_ada[l] + b_ada[l])[:, None, :]
        y_p, k_l, v_l = _layer(y_p, mod_ctx, _context_attention, *shared)
        new_k.append(k_l)
        new_v.append(v_l)
        attend_lat = functools.partial(_neighbourhood_attention, ck=cache_k[:, l],
                                       cv=cache_v[:, l], rpb=rpb[l])
        y_s, _, _ = _layer(y_s, mod_lat, attend_lat, *shared)
    new_cache_k = jnp.stack(new_k, axis=1)
    new_cache_v = jnp.stack(new_v, axis=1)
    return (y_p, y_s, new_cache_k, new_cache_v)
```

```python
import functools
import math

import numpy as np
import jax
import jax.numpy as jnp
from jax import lax
from jax.experimental import pallas as pl
from jax.experimental.pallas import tpu as pltpu

F32 = jnp.float32
BF16 = jnp.bfloat16

RMS_EPS = 1e-6
MASKED_SCORE = -1e30

HEAD_DIM = 128
GROUP_DIM = 128
GRID_W = 64
CHUNK = 128
MAX_WIN_R = 8
WIN_C = 16
N_MOD = 6
MOD_SHIFT1, MOD_SCALE1, MOD_GATE1, MOD_SHIFT2, MOD_SCALE2, MOD_GATE2 = range(N_MOD)
MOD_ROWS = 8

MIB = 2 ** 20


def _params(semantics, vmem_mib):
    return pltpu.CompilerParams(dimension_semantics=semantics, vmem_limit_bytes=vmem_mib * MIB)


def _rms(x, g):
    return x * lax.rsqrt(jnp.mean(x * x, axis=-1, keepdims=True) + RMS_EPS) * g


def _dot(a, b):
    return jnp.dot(a, b, preferred_element_type=F32)


def _dot_nt(a, b):
    return lax.dot_general(a, b, (((1,), (1,)), ((), ())), preferred_element_type=F32)


def _split_bf16(a):
    hi = a.astype(BF16)
    return hi, (a - hi.astype(F32)).astype(BF16)


def _ada_kernel(c_ref, w_ref, b_ref, o_ref):
    c = c_ref[...]
    s = c * jax.nn.sigmoid(c)
    o_ref[...] = _dot(s.astype(BF16), w_ref[...].astype(BF16)) + b_ref[...]


def _ada(cvec, w_ada, b_ada):
    depth, d, width = w_ada.shape
    tn = 1024
    return pl.pallas_call(
        _ada_kernel,
        grid=(depth, width // tn),
        in_specs=[
            pl.BlockSpec((MOD_ROWS, d), lambda l, n: (0, 0)),
            pl.BlockSpec((None, d, tn), lambda l, n: (l, 0, n)),
            pl.BlockSpec((None, 1, tn), lambda l, n: (l, 0, n)),
        ],
        out_specs=pl.BlockSpec((None, MOD_ROWS, tn), lambda l, n: (l, 0, n)),
        out_shape=jax.ShapeDtypeStruct((depth, MOD_ROWS, width), F32),
        compiler_params=_params(("parallel", "parallel"), 40),
    )(cvec, w_ada, b_ada.reshape(depth, 1, width))


class _Rows:
    def __init__(self, m_ctx, dec_batch, dec_seq, d):
        self.m_ctx, self.dec_batch, self.dec_seq, self.d = m_ctx, dec_batch, dec_seq, d
        self.m = m_ctx + dec_batch * dec_seq

    def group(self, i, tm):
        n_ctx = self.m_ctx // tm
        per_batch = self.dec_seq // tm
        return jnp.where(i < n_ctx, 0, 1 + (i - n_ctx) // per_batch)

    def mod_spec(self, layer, piece, tm):
        return pl.BlockSpec((None, None, None, 1, self.d),
                            lambda i, *_: (layer, piece, self.group(i, tm), 0, 0))

    def gain_spec(self, layer):
        return pl.BlockSpec((None, 1, self.d), lambda i, *_: (layer, 0, 0))


def _prologue_kernel(x_ref, g_ref, sc_ref, sh_ref, h_ref):
    h = _rms(x_ref[...], g_ref[...]) * (1.0 + sc_ref[...]) + sh_ref[...]
    h_ref[...] = h.astype(h_ref.dtype)


def _prologue(rows, x, g_pre, mods, layer):
    tm = 512
    return pl.pallas_call(
        _prologue_kernel,
        grid=(rows.m // tm,),
        in_specs=[
            pl.BlockSpec((tm, rows.d), lambda i: (i, 0)),
            rows.gain_spec(layer),
            rows.mod_spec(layer, MOD_SCALE1, tm),
            rows.mod_spec(layer, MOD_SHIFT1, tm),
        ],
        out_specs=pl.BlockSpec((tm, rows.d), lambda i: (i, 0)),
        out_shape=jax.ShapeDtypeStruct((rows.m, rows.d), BF16),
        compiler_params=_params(("parallel",), 32),
    )(x, g_pre, mods, mods)


def _matmul_kernel(a_ref, w_ref, o_ref):
    o_ref[...] = _dot(a_ref[...], w_ref[...]).astype(o_ref.dtype)


def _project(h, w, layer, col_start, width, tn, out_dtype):
    m, k = h.shape
    tm = 1024
    col_block = col_start // tn
    assert col_block * tn == col_start and width % tn == 0
    return pl.pallas_call(
        _matmul_kernel,
        grid=(width // tn, m // tm),
        in_specs=[
            pl.BlockSpec((tm, k), lambda n, i: (i, 0)),
            pl.BlockSpec((None, k, tn), lambda n, i: (layer, 0, col_block + n)),
        ],
        out_specs=pl.BlockSpec((tm, tn), lambda n, i: (i, n)),
        out_shape=jax.ShapeDtypeStruct((m, width), out_dtype),
        compiler_params=_params(("parallel", "parallel"), 48),
    )(h, w)


def _ctx_attn_kernel(q_ref, k_ref, v_ref, o_ref, *, n_heads, scale):
    for h in range(n_heads):
        cols = slice(h * HEAD_DIM, (h + 1) * HEAD_DIM)
        q = q_ref[:, cols].astype(BF16)
        k = k_ref[:, cols].astype(BF16)
        v = v_ref[:, cols].astype(BF16)
        s = _dot_nt(q, k) * scale
        p = jnp.exp(s - jnp.max(s, axis=-1, keepdims=True))
        denom = jnp.sum(p, axis=-1, keepdims=True)
        o_ref[:, cols] = (_dot(p.astype(BF16), v) / denom).astype(o_ref.dtype)


def _ctx_attention(qkv, batch, seq, w_a):
    n_heads = w_a // HEAD_DIM
    kern = functools.partial(_ctx_attn_kernel, n_heads=n_heads, scale=1.0 / math.sqrt(HEAD_DIM))
    return pl.pallas_call(
        kern,
        grid=(batch,),
        in_specs=[pl.BlockSpec((seq, w_a), lambda b, j=j: (b, j)) for j in range(3)],
        out_specs=pl.BlockSpec((seq, w_a), lambda b: (b, 0)),
        out_shape=jax.ShapeDtypeStruct((batch * seq, w_a), BF16),
        compiler_params=_params(("parallel",), 32),
    )(qkv, qkv, qkv)


def _window_start(r, rows):
    win_r = min(MAX_WIN_R, rows)
    return min(max(r - win_r // 2, 0), rows - win_r)


def _bias_variants(rows):
    variant_of, reps, seen = [], [], {}
    for r in range(rows):
        key = _window_start(r, rows) - r
        if key not in seen:
            seen[key] = len(reps)
            reps.append(r)
        variant_of.append(seen[key])
    return variant_of, reps


def _bias_kernel(rpb_ref, o_ref, *, rows):
    lh = pl.program_id(0)
    win_r = min(MAX_WIN_R, rows)
    n_dc = 2 * WIN_C - 1
    q = lax.broadcasted_iota(jnp.int32, (GRID_W, GRID_W), 0)
    kc = lax.broadcasted_iota(jnp.int32, (GRID_W, GRID_W), 1)
    dc = jnp.clip(kc - q, -(WIN_C - 1), WIN_C - 1) + (WIN_C - 1)
    c_start = jnp.clip(q - WIN_C // 2, 0, GRID_W - WIN_C)
    valid = (kc >= c_start) & (kc < c_start + WIN_C)
    tables = []
    for dr in range(2 * MAX_WIN_R - 1):
        t = jnp.zeros((GRID_W, GRID_W), F32)
        for d in range(n_dc):
            t = jnp.where(dc == d, rpb_ref[lh, dr * n_dc + d], t)
        tables.append(jnp.where(valid, t, MASKED_SCORE))
    _, reps = _bias_variants(rows)
    for var, r in enumerate(reps):
        for i in range(win_r):
            dr = _window_start(r, rows) + i - r + (MAX_WIN_R - 1)
            o_ref[var, :, i * GRID_W:(i + 1) * GRID_W] = tables[dr]


def _window_bias(rpb, rows):
    depth, n_heads, n_dr, n_dc = rpb.shape
    win_r = min(MAX_WIN_R, rows)
    n_var = len(_bias_variants(rows)[1])
    return pl.pallas_call(
        functools.partial(_bias_kernel, rows=rows),
        grid=(depth * n_heads,),
        in_specs=[pl.BlockSpec(memory_space=pltpu.SMEM)],
        out_specs=pl.BlockSpec((None, n_var, GRID_W, win_r * GRID_W), lambda i: (i, 0, 0, 0)),
        out_shape=jax.ShapeDtypeStruct((depth * n_heads, n_var, GRID_W, win_r * GRID_W), F32),
        compiler_params=_params(("parallel",), 32),
    )(rpb.reshape(depth * n_heads, n_dr * n_dc))


def _lat_attn_kernel(q_ref, k_ref, v_ref, ck_ref, cv_ref, bias_ref, o_ref, *, rows, scale):
    win_r = min(MAX_WIN_R, rows)
    variant_of, _ = _bias_variants(rows)
    ck = ck_ref[...].astype(BF16)
    cv = cv_ref[...].astype(BF16)
    for r in range(rows):
        q = q_ref[r * GRID_W:(r + 1) * GRID_W, :].astype(BF16)
        w0 = _window_start(r, rows) * GRID_W
        kw = k_ref[w0:w0 + win_r * GRID_W, :].astype(BF16)
        vw = v_ref[w0:w0 + win_r * GRID_W, :].astype(BF16)
        s_lat = _dot_nt(q, kw) * scale + bias_ref[variant_of[r]]
        s_ctx = _dot_nt(q, ck) * scale
        top = jnp.maximum(jnp.max(s_lat, axis=-1, keepdims=True),
                          jnp.max(s_ctx, axis=-1, keepdims=True))
        p_lat = jnp.exp(s_lat - top)
        p_ctx = jnp.exp(s_ctx - top)
        denom = jnp.sum(p_lat, axis=-1, keepdims=True) + jnp.sum(p_ctx, axis=-1, keepdims=True)
        o = _dot(p_lat.astype(BF16), vw) + _dot(p_ctx.astype(BF16), cv)
        o_ref[r * GRID_W:(r + 1) * GRID_W, :] = (o / denom).astype(o_ref.dtype)


def _lat_attention(qkv, cache_k, cache_v, bias, layer, m_ctx, dec_batch, dec_seq, w_a):
    n_heads = w_a // HEAD_DIM
    rows = dec_seq // GRID_W
    past = cache_k.shape[2]
    first = m_ctx // dec_seq
    assert first * dec_seq == m_ctx
    kern = functools.partial(_lat_attn_kernel, rows=rows, scale=1.0 / math.sqrt(HEAD_DIM))
    qkv_specs = [pl.BlockSpec((dec_seq, HEAD_DIM), lambda b, h, j=j: (first + b, j * n_heads + h))
                 for j in range(3)]
    cache_spec = pl.BlockSpec((None, None, past, HEAD_DIM), lambda b, h: (b, layer, 0, h))
    n_var = bias.shape[1]
    return pl.pallas_call(
        kern,
        grid=(dec_batch, n_heads),
        in_specs=qkv_specs + [
            cache_spec, cache_spec,
            pl.BlockSpec((None, n_var) + bias.shape[2:], lambda b, h: (layer * n_heads + h, 0, 0, 0)),
        ],
        out_specs=pl.BlockSpec((dec_seq, HEAD_DIM), lambda b, h: (b, h)),
        out_shape=jax.ShapeDtypeStruct((dec_batch * dec_seq, w_a), BF16),
        compiler_params=_params(("parallel", "parallel"), 32),
    )(qkv, qkv, qkv, cache_k, cache_v, bias)


def _dft_tables(n):
    idx = np.arange(n, dtype=np.int64)
    ang = 2.0 * np.pi * ((idx[:, None] * idx[None, :]) % n) / n
    return np.cos(ang), np.sin(ang)


def _np_split_bf16(a):
    a32 = np.asarray(a, np.float32)
    hi = a32.astype(BF16)
    lo = (a32 - hi.astype(np.float32)).astype(BF16)
    return jnp.asarray(hi), jnp.asarray(lo)


def _fourier_kernel(f_ref, wc_hi_ref, wc_lo_ref, cn_hi_ref, cn_lo_ref, sn_hi_ref, sn_lo_ref,
                    o_ref, *, n_groups, scale):
    wc_hi, wc_lo = wc_hi_ref[...], wc_lo_ref[...]
    pc, ps = [], []
    for g in range(n_groups):
        x_hi, x_lo = _split_bf16(f_ref[:, g * GROUP_DIM:(g + 1) * GROUP_DIM])
        p = _dot(x_hi, wc_hi) + _dot(x_lo, wc_hi) + _dot(x_hi, wc_lo)
        pc.append(p[:, :GROUP_DIM])
        ps.append(p[:, GROUP_DIM:])

    def left3(m_hi_ref, m_lo_ref, b):
        b_hi, b_lo = _split_bf16(b)
        m_hi = m_hi_ref[...]
        return _dot(m_hi, b_hi) + _dot(m_lo_ref[...], b_hi) + _dot(m_hi, b_lo)

    y = (left3(cn_hi_ref, cn_lo_ref, jnp.concatenate(pc, axis=1))
         - left3(sn_hi_ref, sn_lo_ref, jnp.concatenate(ps, axis=1)))
    o_ref[...] = (y * scale).astype(o_ref.dtype)


def _fourier(f, n_batch, n_pos, first_block):
    w_f = f.shape[1]
    cc, sc = _dft_tables(GROUP_DIM)
    wc_hi, wc_lo = _np_split_bf16(np.concatenate([cc, sc], axis=1))
    cn, sn = _dft_tables(n_pos)
    cn_hi, cn_lo = _np_split_bf16(cn)
    sn_hi, sn_lo = _np_split_bf16(sn)
    const = lambda a: pl.BlockSpec(a.shape, lambda b: (0, 0))
    kern = functools.partial(_fourier_kernel, n_groups=w_f // GROUP_DIM,
                             scale=1.0 / math.sqrt(n_pos * GROUP_DIM))
    tables = (wc_hi, wc_lo, cn_hi, cn_lo, sn_hi, sn_lo)
    return pl.pallas_call(
        kern,
        grid=(n_batch,),
        in_specs=[pl.BlockSpec((n_pos, w_f), lambda b: (first_block + b, 0))]
                 + [const(t) for t in tables],
        out_specs=pl.BlockSpec((n_pos, w_f), lambda b: (b, 0)),
        out_shape=jax.ShapeDtypeStruct((n_batch * n_pos, w_f), BF16),
        compiler_params=_params(("parallel",), 48),
    )(f, *tables)


def _sgu_kernel(uv_ref, g_ref, w_ref, b_ref, o_ref, *, n_groups, n_chunks):
    w_c = n_groups * GROUP_DIM
    uv = uv_ref[...]
    uv = 0.5 * uv * (1.0 + lax.erf(uv * math.sqrt(0.5)))
    for g in range(n_groups):
        cols = slice(g * GROUP_DIM, (g + 1) * GROUP_DIM)
        v = _rms(uv[:, w_c + g * GROUP_DIM:w_c + (g + 1) * GROUP_DIM], g_ref[:, cols]).astype(BF16)
        v_wide = jnp.concatenate([v[c * CHUNK:(c + 1) * CHUNK, :] for c in range(n_chunks)], axis=1)
        s = _dot(w_ref[g], v_wide) + b_ref[:, g:g + 1]
        for c in range(n_chunks):
            u = uv[c * CHUNK:(c + 1) * CHUNK, cols]
            o_ref[c * CHUNK:(c + 1) * CHUNK, cols] = (
                u * s[:, c * GROUP_DIM:(c + 1) * GROUP_DIM]).astype(o_ref.dtype)


def _spatial_gating(uv, g_sgu, w_sp, b_sp_t, layer):
    m, two_wc = uv.shape
    w_c = two_wc // 2
    n_groups = w_c // GROUP_DIM
    n_chunks = 4
    tm = n_chunks * CHUNK
    kern = functools.partial(_sgu_kernel, n_groups=n_groups, n_chunks=n_chunks)
    return pl.pallas_call(
        kern,
        grid=(m // tm,),
        in_specs=[
            pl.BlockSpec((tm, two_wc), lambda i: (i, 0)),
            pl.BlockSpec((None, 1, w_c), lambda i: (layer, 0, 0)),
            pl.BlockSpec((None, n_groups, CHUNK, CHUNK), lambda i: (layer, 0, 0, 0)),
            pl.BlockSpec((None, CHUNK, n_groups), lambda i: (layer, 0, 0)),
        ],
        out_specs=pl.BlockSpec((tm, w_c), lambda i: (i, 0)),
        out_shape=jax.ShapeDtypeStruct((m, w_c), BF16),
        compiler_params=_params(("parallel",), 32),
    )(uv, g_sgu, w_sp, b_sp_t)


def _merge_kernel(oa_ref, of_ref, oc_ref, wa_ref, wf_ref, wc_ref, ga_ref, gf_ref, gc_ref, o_ref):
    acc = jax.nn.sigmoid(ga_ref[...]) * _dot(oa_ref[...], wa_ref[...])
    acc += jax.nn.sigmoid(gf_ref[...]) * _dot(of_ref[...], wf_ref[...])
    acc += jax.nn.sigmoid(gc_ref[...]) * _dot(oc_ref[...], wc_ref[...])
    o_ref[...] = acc.astype(o_ref.dtype)


def _merge(o_a, o_f, o_c, w_br_a, w_br_f, w_br_c, gates, layer):
    m = o_a.shape[0]
    d = w_br_a.shape[2]
    tm, tn = 1024, 512
    nb = d // tn
    act = lambda a: pl.BlockSpec((tm, a.shape[1]), lambda i, n: (i, 0))
    wgt = lambda w: pl.BlockSpec((None, w.shape[1], tn), lambda i, n: (layer, 0, n))
    gate = lambda j: pl.BlockSpec((tm, tn), lambda i, n: (i, j * nb + n))
    return pl.pallas_call(
        _merge_kernel,
        grid=(m // tm, nb),
        in_specs=[act(o_a), act(o_f), act(o_c), wgt(w_br_a), wgt(w_br_f), wgt(w_br_c),
                  gate(0), gate(1), gate(2)],
        out_specs=pl.BlockSpec((tm, tn), lambda i, n: (i, n)),
        out_shape=jax.ShapeDtypeStruct((m, d), BF16),
        compiler_params=_params(("parallel", "parallel"), 48),
    )(o_a, o_f, o_c, w_br_a, w_br_f, w_br_c, gates, gates, gates)


def _outproj_kernel(mg_ref, w_ref, x_ref, gpost_ref, gt_ref, gpre_ref, sc_ref, sh_ref,
                    xo_ref, ho_ref):
    y = _dot(mg_ref[...], w_ref[...])
    x1 = x_ref[...] + gt_ref[...] * _rms(y, gpost_ref[...])
    xo_ref[...] = x1
    h = _rms(x1, gpre_ref[...]) * (1.0 + sc_ref[...]) + sh_ref[...]
    ho_ref[...] = h.astype(ho_ref.dtype)


def _out_projection(rows, merged, w_out, x, g_post_mix, g_pre_mlp, mods, layer):
    tm = 512
    d = rows.d
    row_spec = pl.BlockSpec((tm, d), lambda i: (i, 0))
    return pl.pallas_call(
        _outproj_kernel,
        grid=(rows.m // tm,),
        in_specs=[
            row_spec,
            pl.BlockSpec((None, d, d), lambda i: (layer, 0, 0), pipeline_mode=pl.Buffered(1)),
            row_spec,
            rows.gain_spec(layer),
            rows.mod_spec(layer, MOD_GATE1, tm),
            rows.gain_spec(layer),
            rows.mod_spec(layer, MOD_SCALE2, tm),
            rows.mod_spec(layer, MOD_SHIFT2, tm),
        ],
        out_specs=[row_spec, row_spec],
        out_shape=[jax.ShapeDtypeStruct((rows.m, d), F32), jax.ShapeDtypeStruct((rows.m, d), BF16)],
        compiler_params=_params(("parallel",), 52),
    )(merged, w_out, x, g_post_mix, mods, g_pre_mlp, mods, mods)


def _mlp_kernel(*refs, n_ff_steps, has_next):
    if has_next:
        (h_ref, w1_ref, w2_ref, x_ref, gpost_ref, gt_ref, gpre_ref, sc_ref, sh_ref,
         xo_ref, ho_ref, acc_ref) = refs
    else:
        h_ref, w1_ref, w2_ref, x_ref, gpost_ref, gt_ref, xo_ref, acc_ref = refs
    step = pl.program_id(1)
    hid = jnp.square(jnp.maximum(_dot(h_ref[...], w1_ref[...]), 0.0)).astype(BF16)
    part = _dot(hid, w2_ref[...])

    @pl.when(step == 0)
    def _():
        acc_ref[...] = part

    @pl.when(step > 0)
    def _():
        acc_ref[...] += part

    @pl.when(step == n_ff_steps - 1)
    def _():
        x2 = x_ref[...] + gt_ref[...] * _rms(acc_ref[...], gpost_ref[...])
        xo_ref[...] = x2
        if has_next:
            h = _rms(x2, gpre_ref[...]) * (1.0 + sc_ref[...]) + sh_ref[...]
            ho_ref[...] = h.astype(ho_ref.dtype)


def _mlp(rows, h2, w1, w2, x, g_post_mlp, g_pre_next, mods, layer, has_next):
    tm, tf = 512, 512
    d = rows.d
    d_ff = w1.shape[2]
    n_ff_steps = d_ff // tf
    row_spec = pl.BlockSpec((tm, d), lambda i, f: (i, 0))
    in_specs = [
        row_spec,
        pl.BlockSpec((None, d, tf), lambda i, f: (layer, 0, f)),
        pl.BlockSpec((None, tf, d), lambda i, f: (layer, f, 0)),
        row_spec,
        rows.gain_spec(layer),
        rows.mod_spec(layer, MOD_GATE2, tm),
    ]
    args = [h2, w1, w2, x, g_post_mlp, mods]
    out_specs = [row_spec]
    out_shape = [jax.ShapeDtypeStruct((rows.m, d), F32)]
    if has_next:
        in_specs += [rows.gain_spec(layer + 1),
                     rows.mod_spec(layer + 1, MOD_SCALE1, tm),
                     rows.mod_spec(layer + 1, MOD_SHIFT1, tm)]
        args += [g_pre_next, mods, mods]
        out_specs.append(row_spec)
        out_shape.append(jax.ShapeDtypeStruct((rows.m, d), BF16))
    outs = pl.pallas_call(
        functools.partial(_mlp_kernel, n_ff_steps=n_ff_steps, has_next=has_next),
        grid=(rows.m // tm, n_ff_steps),
        in_specs=in_specs,
        out_specs=out_specs,
        out_shape=out_shape,
        scratch_shapes=[pltpu.VMEM((tm, d), F32)],
        compiler_params=_params(("parallel", "arbitrary"), 52),
    )(*args)
    return (outs[0], outs[1]) if has_next else (outs[0], None)


def kernel(x_prompt, x_sample, cache_k, cache_v, c, c_ctx, w_ada, b_ada, g_pre_mix, g_post_mix,
           g_pre_mlp, g_post_mlp, w_in, rpb, g_sgu, w_spatial, b_spatial, w_br_a, w_br_f, w_br_c,
           w_out, w_mlp1, w_mlp2):
    batch, seq, d = x_prompt.shape
    dec_batch, dec_seq, _ = x_sample.shape
    depth = w_in.shape[0]
    past = cache_k.shape[2]
    w_a = w_br_a.shape[1]
    w_f = w_br_f.shape[1]
    w_c = w_br_c.shape[1]
    n_heads = w_a // HEAD_DIM
    m_ctx = batch * seq
    rows = _Rows(m_ctx, dec_batch, dec_seq, d)
    assert dec_seq % GRID_W == 0 and m_ctx % dec_seq == 0 and MOD_ROWS >= 1 + dec_batch

    cvec = jnp.zeros((MOD_ROWS, d), F32).at[0].set(c_ctx).at[1:1 + dec_batch].set(c)
    mods = _ada(cvec, w_ada, b_ada)
    mods = mods[:, :1 + dec_batch].reshape(depth, 1 + dec_batch, N_MOD, d)
    mods = mods.transpose(0, 2, 1, 3)[:, :, :, None, :]

    w_in_b = w_in.astype(BF16)
    w_br_a_b, w_br_f_b, w_br_c_b = w_br_a.astype(BF16), w_br_f.astype(BF16), w_br_c.astype(BF16)
    w_out_b, w1_b, w2_b = w_out.astype(BF16), w_mlp1.astype(BF16), w_mlp2.astype(BF16)
    w_sp_b = w_spatial.astype(BF16)
    b_sp_t = b_spatial.transpose(0, 2, 1)
    g_sgu_row = g_sgu.reshape(depth, 1, w_c)
    gains = [g.reshape(depth, 1, d) for g in (g_pre_mix, g_post_mix, g_pre_mlp, g_post_mlp)]
    g_pre_mix_r, g_post_mix_r, g_pre_mlp_r, g_post_mlp_r = gains
    cache_k_r = cache_k.reshape(dec_batch, depth, past, w_a)
    cache_v_r = cache_v.reshape(dec_batch, depth, past, w_a)
    bias = _window_bias(rpb, dec_seq // GRID_W)

    col_f = 3 * w_a
    col_uv = col_f + w_f
    col_gates = col_uv + 2 * w_c

    x = jnp.concatenate([x_prompt.reshape(m_ctx, d), x_sample.reshape(dec_batch * dec_seq, d)])
    h = _prologue(rows, x, g_pre_mix_r, mods, 0)
    new_k, new_v = [], []
    for l in range(depth):
        qkv = _project(h, w_in_b, l, 0, 3 * w_a, 1024, F32)
        f = _project(h, w_in_b, l, col_f, w_f, 512, F32)
        uv = _project(h, w_in_b, l, col_uv, 2 * w_c, 512, F32)
        gates = _project(h, w_in_b, l, col_gates, 3 * d, 1536, F32)
        new_k.append(qkv[:m_ctx, w_a:2 * w_a].reshape(batch, seq, n_heads, HEAD_DIM))
        new_v.append(qkv[:m_ctx, 2 * w_a:].reshape(batch, seq, n_heads, HEAD_DIM))

        o_a = jnp.concatenate([
            _ctx_attention(qkv, batch, seq, w_a),
            _lat_attention(qkv, cache_k_r, cache_v_r, bias, l, m_ctx, dec_batch, dec_seq, w_a)])
        o_f = jnp.concatenate([
            _fourier(f, batch, seq, 0),
            _fourier(f, dec_batch, dec_seq, m_ctx // dec_seq)])
        o_c = _spatial_gating(uv, g_sgu_row, w_sp_b, b_sp_t, l)

        merged = _merge(o_a, o_f, o_c, w_br_a_b, w_br_f_b, w_br_c_b, gates, l)
        x, h2 = _out_projection(rows, merged, w_out_b, x, g_post_mix_r, g_pre_mlp_r, mods, l)
        x, h = _mlp(rows, h2, w1_b, w2_b, x, g_post_mlp_r, g_pre_mix_r, mods, l, l + 1 < depth)

    y_p = x[:m_ctx].reshape(batch, seq, d)
    y_s = x[m_ctx:].reshape(dec_batch, dec_seq, d)
    return (y_p, y_s, jnp.stack(new_k, axis=1), jnp.stack(new_v, axis=1))
```

```python
import functools
import math

import numpy as np
import jax
import jax.numpy as jnp
from jax import lax
from jax.experimental import pallas as pl
from jax.experimental.pallas import tpu as pltpu

F32 = jnp.float32
BF16 = jnp.bfloat16

RMS_EPS = 1e-6
MASKED_SCORE = -1e30

HEAD_DIM = 128
GROUP_DIM = 128
GRID_W = 64
CHUNK = 128
MAX_WIN_R = 8
WIN_C = 16
N_MOD = 6
MOD_SHIFT1, MOD_SCALE1, MOD_GATE1, MOD_SHIFT2, MOD_SCALE2, MOD_GATE2 = range(N_MOD)
MOD_ROWS = 8

MIB = 2 ** 20


def _params(semantics, vmem_mib):
    return pltpu.CompilerParams(dimension_semantics=semantics, vmem_limit_bytes=vmem_mib * MIB)


def _rms(x, g):
    return x * lax.rsqrt(jnp.mean(x * x, axis=-1, keepdims=True) + RMS_EPS) * g


def _dot(a, b):
    return jnp.dot(a, b, preferred_element_type=F32)


def _dot_nt(a, b):
    return lax.dot_general(a, b, (((1,), (1,)), ((), ())), preferred_element_type=F32)


def _split_bf16(a):
    hi = a.astype(BF16)
    return hi, (a - hi.astype(F32)).astype(BF16)


def _ada_kernel(c_ref, w_ref, b_ref, o_ref):
    c = c_ref[...]
    s = c * jax.nn.sigmoid(c)
    o_ref[...] = _dot(s.astype(BF16), w_ref[...].astype(BF16)) + b_ref[...]


def _ada(cvec, w_ada, b_ada):
    depth, d, width = w_ada.shape
    tn = 1024
    return pl.pallas_call(
        _ada_kernel,
        name="ada",
        grid=(depth, width // tn),
        in_specs=[
            pl.BlockSpec((MOD_ROWS, d), lambda l, n: (0, 0)),
            pl.BlockSpec((None, d, tn), lambda l, n: (l, 0, n)),
            pl.BlockSpec((None, 1, tn), lambda l, n: (l, 0, n)),
        ],
        out_specs=pl.BlockSpec((None, MOD_ROWS, tn), lambda l, n: (l, 0, n)),
        out_shape=jax.ShapeDtypeStruct((depth, MOD_ROWS, width), F32),
        compiler_params=_params(("parallel", "parallel"), 40),
    )(cvec, w_ada, b_ada.reshape(depth, 1, width))


class _Rows:
    def __init__(self, m_ctx, dec_batch, dec_seq, d):
        self.m_ctx, self.dec_batch, self.dec_seq, self.d = m_ctx, dec_batch, dec_seq, d
        self.m_lat = dec_batch * dec_seq
        self.m = m_ctx + self.m_lat

    def group(self, i, tm):
        n_ctx = self.m_ctx // tm
        per_batch = self.dec_seq // tm
        return jnp.where(i < n_ctx, 0, 1 + (i - n_ctx) // per_batch)

    def mod_spec(self, layer, piece, tm):
        return pl.BlockSpec((None, None, None, 1, self.d),
                            lambda i, *_: (layer, piece, self.group(i, tm), 0, 0))

    def gain_spec(self, layer):
        return pl.BlockSpec((None, 1, self.d), lambda i, *_: (layer, 0, 0))

    def split_specs(self, tm, width):
        n_ctx = self.m_ctx // tm
        ctx = pl.BlockSpec((tm, width), lambda i, *_: (jnp.minimum(i, n_ctx - 1), 0))
        lat = pl.BlockSpec((tm, width), lambda i, *_: (jnp.maximum(i - n_ctx, 0), 0))
        return ctx, lat

    def pick(self, tm, ctx_ref, lat_ref):
        return jnp.where(pl.program_id(0) < self.m_ctx // tm, ctx_ref[...], lat_ref[...])


def _prologue_kernel(xp_ref, xs_ref, g_ref, sc_ref, sh_ref, x_ref, h_ref, *, rows, tm):
    x = rows.pick(tm, xp_ref, xs_ref)
    x_ref[...] = x
    h = _rms(x, g_ref[...]) * (1.0 + sc_ref[...]) + sh_ref[...]
    h_ref[...] = h.astype(h_ref.dtype)


def _prologue(rows, x_ctx, x_lat, g_pre, mods, layer):
    tm = 512
    row_spec = pl.BlockSpec((tm, rows.d), lambda i: (i, 0))
    return pl.pallas_call(
        functools.partial(_prologue_kernel, rows=rows, tm=tm),
        name="prologue",
        grid=(rows.m // tm,),
        in_specs=[
            *rows.split_specs(tm, rows.d),
            rows.gain_spec(layer),
            rows.mod_spec(layer, MOD_SCALE1, tm),
            rows.mod_spec(layer, MOD_SHIFT1, tm),
        ],
        out_specs=[row_spec, row_spec],
        out_shape=[jax.ShapeDtypeStruct((rows.m, rows.d), F32),
                   jax.ShapeDtypeStruct((rows.m, rows.d), BF16)],
        compiler_params=_params(("arbitrary",), 40),
    )(x_ctx, x_lat, g_pre, mods, mods)


def _matmul_kernel(a_ref, w_ref, o_ref):
    o_ref[...] = _dot(a_ref[...], w_ref[...]).astype(o_ref.dtype)


def _project(h, w, layer, width, tn, out_dtype):
    m, k = h.shape
    tm = 1024
    assert width % tn == 0
    return pl.pallas_call(
        _matmul_kernel,
        name="in_proj",
        grid=(width // tn, m // tm),
        in_specs=[
            pl.BlockSpec((tm, k), lambda n, i: (i, 0)),
            pl.BlockSpec((None, k, tn), lambda n, i: (layer, 0, n)),
        ],
        out_specs=pl.BlockSpec((tm, tn), lambda n, i: (i, n)),
        out_shape=jax.ShapeDtypeStruct((m, width), out_dtype),
        compiler_params=_params(("parallel", "parallel"), 48),
    )(h, w)


def _ctx_attn_kernel(q_ref, k_ref, v_ref, o_ref, ko_ref, vo_ref, *, n_heads, scale):
    ko_ref[...] = k_ref[...]
    vo_ref[...] = v_ref[...]
    for h in range(n_heads):
        cols = slice(h * HEAD_DIM, (h + 1) * HEAD_DIM)
        q = q_ref[:, cols].astype(BF16)
        k = k_ref[:, cols].astype(BF16)
        v = v_ref[:, cols].astype(BF16)
        s = _dot_nt(q, k) * scale
        p = jnp.exp(s - jnp.max(s, axis=-1, keepdims=True))
        denom = jnp.sum(p, axis=-1, keepdims=True)
        o_ref[:, cols] = (_dot(p.astype(BF16), v) / denom).astype(o_ref.dtype)


def _ctx_attention(proj, new_k, new_v, layer, depth, batch, seq, w_a):
    n_heads = w_a // HEAD_DIM
    kern = functools.partial(_ctx_attn_kernel, n_heads=n_heads, scale=1.0 / math.sqrt(HEAD_DIM))
    cache_spec = pl.BlockSpec((None, None, seq, w_a), lambda b: (b, layer, 0, 0))
    cache_shape = jax.ShapeDtypeStruct((batch, depth, seq, w_a), F32)
    in_specs = [pl.BlockSpec((seq, w_a), lambda b, j=j: (b, j)) for j in range(3)]
    args = [proj, proj, proj]
    aliases = {}
    if new_k is not None:
        in_specs += [pl.BlockSpec(memory_space=pl.ANY)] * 2
        args += [new_k, new_v]
        aliases = {3: 1, 4: 2}

    def body(q_ref, k_ref, v_ref, *rest):
        kern(q_ref, k_ref, v_ref, *rest[-3:])

    return pl.pallas_call(
        body,
        name="ctx_attn",
        grid=(batch,),
        in_specs=in_specs,
        out_specs=[pl.BlockSpec((seq, w_a), lambda b: (b, 0)), cache_spec, cache_spec],
        out_shape=[jax.ShapeDtypeStruct((batch * seq, w_a), BF16), cache_shape, cache_shape],
        input_output_aliases=aliases,
        compiler_params=_params(("arbitrary",), 32),
    )(*args)


def _window_start(r, rows):
    win_r = min(MAX_WIN_R, rows)
    return min(max(r - win_r // 2, 0), rows - win_r)


def _bias_variants(rows):
    variant_of, reps, seen = [], [], {}
    for r in range(rows):
        key = _window_start(r, rows) - r
        if key not in seen:
            seen[key] = len(reps)
            reps.append(r)
        variant_of.append(seen[key])
    return variant_of, reps


def _bias_kernel(rpb_ref, o_ref, *, rows):
    lh = pl.program_id(0)
    win_r = min(MAX_WIN_R, rows)
    n_dc = 2 * WIN_C - 1
    q = lax.broadcasted_iota(jnp.int32, (GRID_W, GRID_W), 0)
    kc = lax.broadcasted_iota(jnp.int32, (GRID_W, GRID_W), 1)
    dc = jnp.clip(kc - q, -(WIN_C - 1), WIN_C - 1) + (WIN_C - 1)
    c_start = jnp.clip(q - WIN_C // 2, 0, GRID_W - WIN_C)
    valid = (kc >= c_start) & (kc < c_start + WIN_C)
    tables = []
    for dr in range(2 * MAX_WIN_R - 1):
        t = jnp.zeros((GRID_W, GRID_W), F32)
        for d in range(n_dc):
            t = jnp.where(dc == d, rpb_ref[lh, dr * n_dc + d], t)
        tables.append(jnp.where(valid, t, MASKED_SCORE))
    _, reps = _bias_variants(rows)
    for var, r in enumerate(reps):
        for i in range(win_r):
            dr = _window_start(r, rows) + i - r + (MAX_WIN_R - 1)
            o_ref[var, :, i * GRID_W:(i + 1) * GRID_W] = tables[dr]


def _window_bias(rpb, rows):
    depth, n_heads, n_dr, n_dc = rpb.shape
    win_r = min(MAX_WIN_R, rows)
    n_var = len(_bias_variants(rows)[1])
    return pl.pallas_call(
        functools.partial(_bias_kernel, rows=rows),
        name="window_bias",
        grid=(depth * n_heads,),
        in_specs=[pl.BlockSpec(memory_space=pltpu.SMEM)],
        out_specs=pl.BlockSpec((None, n_var, GRID_W, win_r * GRID_W), lambda i: (i, 0, 0, 0)),
        out_shape=jax.ShapeDtypeStruct((depth * n_heads, n_var, GRID_W, win_r * GRID_W), F32),
        compiler_params=_params(("parallel",), 32),
    )(rpb.reshape(depth * n_heads, n_dr * n_dc))


def _lat_attn_kernel(q_ref, k_ref, v_ref, ck_ref, cv_ref, bias_ref, o_ref, *, rows, scale):
    win_r = min(MAX_WIN_R, rows)
    variant_of, _ = _bias_variants(rows)
    ck = ck_ref[...].astype(BF16)
    cv = cv_ref[...].astype(BF16)
    for r in range(rows):
        q = q_ref[r * GRID_W:(r + 1) * GRID_W, :].astype(BF16)
        w0 = _window_start(r, rows) * GRID_W
        kw = k_ref[w0:w0 + win_r * GRID_W, :].astype(BF16)
        vw = v_ref[w0:w0 + win_r * GRID_W, :].astype(BF16)
        s_lat = _dot_nt(q, kw) * scale + bias_ref[variant_of[r]]
        s_ctx = _dot_nt(q, ck) * scale
        top = jnp.maximum(jnp.max(s_lat, axis=-1, keepdims=True),
                          jnp.max(s_ctx, axis=-1, keepdims=True))
        p_lat = jnp.exp(s_lat - top)
        p_ctx = jnp.exp(s_ctx - top)
        denom = jnp.sum(p_lat, axis=-1, keepdims=True) + jnp.sum(p_ctx, axis=-1, keepdims=True)
        o = _dot(p_lat.astype(BF16), vw) + _dot(p_ctx.astype(BF16), cv)
        o_ref[r * GRID_W:(r + 1) * GRID_W, :] = (o / denom).astype(o_ref.dtype)


def _lat_attention(proj, cache_k, cache_v, bias, layer, m_ctx, dec_batch, dec_seq, w_a):
    n_heads = w_a // HEAD_DIM
    rows = dec_seq // GRID_W
    past = cache_k.shape[2]
    first = m_ctx // dec_seq
    assert first * dec_seq == m_ctx
    kern = functools.partial(_lat_attn_kernel, rows=rows, scale=1.0 / math.sqrt(HEAD_DIM))
    qkv_specs = [pl.BlockSpec((dec_seq, HEAD_DIM), lambda b, h, j=j: (first + b, j * n_heads + h))
                 for j in range(3)]
    cache_spec = pl.BlockSpec((None, None, past, HEAD_DIM), lambda b, h: (b, layer, 0, h))
    n_var = bias.shape[1]
    return pl.pallas_call(
        kern,
        name="lat_attn",
        grid=(dec_batch, n_heads),
        in_specs=qkv_specs + [
            cache_spec, cache_spec,
            pl.BlockSpec((None, n_var) + bias.shape[2:], lambda b, h: (layer * n_heads + h, 0, 0, 0)),
        ],
        out_specs=pl.BlockSpec((dec_seq, HEAD_DIM), lambda b, h: (b, h)),
        out_shape=jax.ShapeDtypeStruct((dec_batch * dec_seq, w_a), BF16),
        compiler_params=_params(("parallel", "parallel"), 32),
    )(proj, proj, proj, cache_k, cache_v, bias)


def _dft_tables(n):
    idx = np.arange(n, dtype=np.int64)
    ang = 2.0 * np.pi * ((idx[:, None] * idx[None, :]) % n) / n
    return np.cos(ang), np.sin(ang)


def _np_split_bf16(a):
    a32 = np.asarray(a, np.float32)
    hi = a32.astype(BF16)
    lo = (a32 - hi.astype(np.float32)).astype(BF16)
    return jnp.asarray(hi), jnp.asarray(lo)


def _fourier_kernel(f_ref, wc_hi_ref, wc_lo_ref, cn_hi_ref, cn_lo_ref, sn_hi_ref, sn_lo_ref,
                    o_ref, *, n_groups, scale):
    wc_hi, wc_lo = wc_hi_ref[...], wc_lo_ref[...]
    pc, ps = [], []
    for g in range(n_groups):
        x_hi, x_lo = _split_bf16(f_ref[:, g * GROUP_DIM:(g + 1) * GROUP_DIM])
        p = _dot(x_hi, wc_hi) + _dot(x_lo, wc_hi) + _dot(x_hi, wc_lo)
        pc.append(p[:, :GROUP_DIM])
        ps.append(p[:, GROUP_DIM:])

    def left3(m_hi_ref, m_lo_ref, b):
        b_hi, b_lo = _split_bf16(b)
        m_hi = m_hi_ref[...]
        return _dot(m_hi, b_hi) + _dot(m_lo_ref[...], b_hi) + _dot(m_hi, b_lo)

    y = (left3(cn_hi_ref, cn_lo_ref, jnp.concatenate(pc, axis=1))
         - left3(sn_hi_ref, sn_lo_ref, jnp.concatenate(ps, axis=1)))
    o_ref[...] = (y * scale).astype(o_ref.dtype)


def _fourier(proj, col_start, w_f, n_batch, n_pos, first_block):
    col_block = col_start // w_f
    assert col_block * w_f == col_start
    cc, sc = _dft_tables(GROUP_DIM)
    wc_hi, wc_lo = _np_split_bf16(np.concatenate([cc, sc], axis=1))
    cn, sn = _dft_tables(n_pos)
    cn_hi, cn_lo = _np_split_bf16(cn)
    sn_hi, sn_lo = _np_split_bf16(sn)
    const = lambda a: pl.BlockSpec(a.shape, lambda b: (0, 0))
    kern = functools.partial(_fourier_kernel, n_groups=w_f // GROUP_DIM,
                             scale=1.0 / math.sqrt(n_pos * GROUP_DIM))
    tables = (wc_hi, wc_lo, cn_hi, cn_lo, sn_hi, sn_lo)
    return pl.pallas_call(
        kern,
        name="fourier",
        grid=(n_batch,),
        in_specs=[pl.BlockSpec((n_pos, w_f), lambda b: (first_block + b, col_block))]
                 + [const(t) for t in tables],
        out_specs=pl.BlockSpec((n_pos, w_f), lambda b: (b, 0)),
        out_shape=jax.ShapeDtypeStruct((n_batch * n_pos, w_f), BF16),
        compiler_params=_params(("parallel",), 48),
    )(proj, *tables)


def _gelu(x):
    return 0.5 * x * (1.0 + lax.erf(x * math.sqrt(0.5)))


def _sgu_kernel(u_ref, v_ref, g_ref, w_ref, b_ref, o_ref, *, n_groups, n_chunks):
    for g in range(n_groups):
        cols = slice(g * GROUP_DIM, (g + 1) * GROUP_DIM)
        v = _rms(_gelu(v_ref[:, cols]), g_ref[:, cols]).astype(BF16)
        v_wide = jnp.concatenate([v[c * CHUNK:(c + 1) * CHUNK, :] for c in range(n_chunks)], axis=1)
        s = _dot(w_ref[g], v_wide) + b_ref[:, g:g + 1]
        for c in range(n_chunks):
            rws = slice(c * CHUNK, (c + 1) * CHUNK)
            o_ref[rws, cols] = (_gelu(u_ref[rws, cols])
                                * s[:, c * GROUP_DIM:(c + 1) * GROUP_DIM]).astype(o_ref.dtype)


def _spatial_gating(proj, col_start, w_c, g_sgu, w_sp, b_sp_t, layer):
    m = proj.shape[0]
    n_groups = w_c // GROUP_DIM
    n_chunks = 4
    tm = n_chunks * CHUNK
    col_block = col_start // w_c
    assert col_block * w_c == col_start
    kern = functools.partial(_sgu_kernel, n_groups=n_groups, n_chunks=n_chunks)
    return pl.pallas_call(
        kern,
        name="spatial_gate",
        grid=(m // tm,),
        in_specs=[
            pl.BlockSpec((tm, w_c), lambda i: (i, col_block)),
            pl.BlockSpec((tm, w_c), lambda i: (i, col_block + 1)),
            pl.BlockSpec((None, 1, w_c), lambda i: (layer, 0, 0)),
            pl.BlockSpec((None, n_groups, CHUNK, CHUNK), lambda i: (layer, 0, 0, 0)),
            pl.BlockSpec((None, CHUNK, n_groups), lambda i: (layer, 0, 0)),
        ],
        out_specs=pl.BlockSpec((tm, w_c), lambda i: (i, 0)),
        out_shape=jax.ShapeDtypeStruct((m, w_c), BF16),
        compiler_params=_params(("parallel",), 32),
    )(proj, proj, g_sgu, w_sp, b_sp_t)


def _merge_kernel(h_ref, oa_ctx_ref, oa_lat_ref, of_ctx_ref, of_lat_ref, oc_ref,
                  wga_ref, wgf_ref, wgc_ref, wa_ref, wf_ref, wc_ref, o_ref, *, rows, tm):
    h = h_ref[...]
    o_a = rows.pick(tm, oa_ctx_ref, oa_lat_ref)
    o_f = rows.pick(tm, of_ctx_ref, of_lat_ref)
    acc = jax.nn.sigmoid(_dot(h, wga_ref[...])) * _dot(o_a, wa_ref[...])
    acc += jax.nn.sigmoid(_dot(h, wgf_ref[...])) * _dot(o_f, wf_ref[...])
    acc += jax.nn.sigmoid(_dot(h, wgc_ref[...])) * _dot(oc_ref[...], wc_ref[...])
    o_ref[...] = acc.astype(o_ref.dtype)


def _merge(rows, h, o_a_ctx, o_a_lat, o_f_ctx, o_f_lat, o_c, w_in, col_gates,
           w_br_a, w_br_f, w_br_c, layer):
    d = rows.d
    tm, tn = 1024, 512
    nb = d // tn
    gate_block = col_gates // tn
    assert gate_block * tn == col_gates
    full = lambda width: pl.BlockSpec((tm, width), lambda i, n: (i, 0))
    gate_w = lambda j: pl.BlockSpec((None, d, tn), lambda i, n: (layer, 0, gate_block + j * nb + n))
    br_w = lambda w: pl.BlockSpec((None, w.shape[1], tn), lambda i, n: (layer, 0, n))
    return pl.pallas_call(
        functools.partial(_merge_kernel, rows=rows, tm=tm),
        name="gate_merge",
        grid=(rows.m // tm, nb),
        in_specs=[full(d),
                  *rows.split_specs(tm, o_a_ctx.shape[1]),
                  *rows.split_specs(tm, o_f_ctx.shape[1]),
                  full(o_c.shape[1]),
                  gate_w(0), gate_w(1), gate_w(2),
                  br_w(w_br_a), br_w(w_br_f), br_w(w_br_c)],
        out_specs=pl.BlockSpec((tm, tn), lambda i, n: (i, n)),
        out_shape=jax.ShapeDtypeStruct((rows.m, d), BF16),
        compiler_params=_params(("arbitrary", "arbitrary"), 56),
    )(h, o_a_ctx, o_a_lat, o_f_ctx, o_f_lat, o_c, w_in, w_in, w_in, w_br_a, w_br_f, w_br_c)


def _outproj_kernel(mg_ref, w_ref, x_ref, gpost_ref, gt_ref, gpre_ref, sc_ref, sh_ref,
                    xo_ref, ho_ref):
    y = _dot(mg_ref[...], w_ref[...])
    x1 = x_ref[...] + gt_ref[...] * _rms(y, gpost_ref[...])
    xo_ref[...] = x1
    h = _rms(x1, gpre_ref[...]) * (1.0 + sc_ref[...]) + sh_ref[...]
    ho_ref[...] = h.astype(ho_ref.dtype)


def _out_projection(rows, merged, w_out, x, g_post_mix, g_pre_mlp, mods, layer):
    tm = 512
    d = rows.d
    row_spec = pl.BlockSpec((tm, d), lambda i: (i, 0))
    return pl.pallas_call(
        _outproj_kernel,
        name="out_proj",
        grid=(rows.m // tm,),
        in_specs=[
            row_spec,
            pl.BlockSpec((None, d, d), lambda i: (layer, 0, 0), pipeline_mode=pl.Buffered(1)),
            row_spec,
            rows.gain_spec(layer),
            rows.mod_spec(layer, MOD_GATE1, tm),
            rows.gain_spec(layer),
            rows.mod_spec(layer, MOD_SCALE2, tm),
            rows.mod_spec(layer, MOD_SHIFT2, tm),
        ],
        out_specs=[row_spec, row_spec],
        out_shape=[jax.ShapeDtypeStruct((rows.m, d), F32), jax.ShapeDtypeStruct((rows.m, d), BF16)],
        compiler_params=_params(("parallel",), 52),
    )(merged, w_out, x, g_post_mix, mods, g_pre_mlp, mods, mods)


def _mlp_kernel(*refs, n_ff_steps, has_next):
    if has_next:
        (h_ref, w1_ref, w2_ref, x_ref, gpost_ref, gt_ref, gpre_ref, sc_ref, sh_ref,
         xo_ref, ho_ref) = refs
    else:
        h_ref, w1_ref, w2_ref, x_ref, gpost_ref, gt_ref, xo_ref = refs
    step = pl.program_id(1)

    @pl.when(step == 0)
    def _():
        xo_ref[...] = jnp.zeros_like(xo_ref)

    hid = jnp.square(jnp.maximum(_dot(h_ref[...], w1_ref[...]), 0.0)).astype(BF16)
    xo_ref[...] += _dot(hid, w2_ref[...])

    @pl.when(step == n_ff_steps - 1)
    def _():
        x2 = x_ref[...] + gt_ref[...] * _rms(xo_ref[...], gpost_ref[...])
        xo_ref[...] = x2
        if has_next:
            h = _rms(x2, gpre_ref[...]) * (1.0 + sc_ref[...]) + sh_ref[...]
            ho_ref[...] = h.astype(ho_ref.dtype)


def _mlp(rows, h2, w1, w2, x, g_post_mlp, g_pre_next, mods, layer, has_next):
    tm, tf = 512, 1024
    d = rows.d
    d_ff = w1.shape[2]
    n_ff_steps = d_ff // tf
    row_spec = pl.BlockSpec((tm, d), lambda i, f: (i, 0))
    in_specs = [
        row_spec,
        pl.BlockSpec((None, d, tf), lambda i, f: (layer, 0, f)),
        pl.BlockSpec((None, tf, d), lambda i, f: (layer, f, 0)),
        row_spec,
        rows.gain_spec(layer),
        rows.mod_spec(layer, MOD_GATE2, tm),
    ]
    args = [h2, w1, w2, x, g_post_mlp, mods]
    out_specs = [row_spec]
    out_shape = [jax.ShapeDtypeStruct((rows.m, d), F32)]
    if has_next:
        in_specs += [rows.gain_spec(layer + 1),
                     rows.mod_spec(layer + 1, MOD_SCALE1, tm),
                     rows.mod_spec(layer + 1, MOD_SHIFT1, tm)]
        args += [g_pre_next, mods, mods]
        out_specs.append(row_spec)
        out_shape.append(jax.ShapeDtypeStruct((rows.m, d), BF16))
    outs = pl.pallas_call(
        functools.partial(_mlp_kernel, n_ff_steps=n_ff_steps, has_next=has_next),
        name="mlp",
        grid=(rows.m // tm, n_ff_steps),
        in_specs=in_specs,
        out_specs=out_specs,
        out_shape=out_shape,
        compiler_params=_params(("parallel", "arbitrary"), 56),
    )(*args)
    return (outs[0], outs[1]) if has_next else (outs[0], None)


def kernel(x_prompt, x_sample, cache_k, cache_v, c, c_ctx, w_ada, b_ada, g_pre_mix, g_post_mix,
           g_pre_mlp, g_post_mlp, w_in, rpb, g_sgu, w_spatial, b_spatial, w_br_a, w_br_f, w_br_c,
           w_out, w_mlp1, w_mlp2):
    batch, seq, d = x_prompt.shape
    dec_batch, dec_seq, _ = x_sample.shape
    depth = w_in.shape[0]
    past = cache_k.shape[2]
    w_a = w_br_a.shape[1]
    w_f = w_br_f.shape[1]
    w_c = w_br_c.shape[1]
    n_heads = w_a // HEAD_DIM
    m_ctx = batch * seq
    rows = _Rows(m_ctx, dec_batch, dec_seq, d)
    assert dec_seq % GRID_W == 0 and m_ctx % dec_seq == 0 and MOD_ROWS >= 1 + dec_batch

    cvec = jnp.zeros((MOD_ROWS, d), F32).at[0].set(c_ctx).at[1:1 + dec_batch].set(c)
    mods = _ada(cvec, w_ada, b_ada)
    mods = mods[:, :1 + dec_batch].reshape(depth, 1 + dec_batch, N_MOD, d)
    mods = mods.transpose(0, 2, 1, 3)[:, :, :, None, :]

    w_in_b = w_in.astype(BF16)
    w_br_a_b, w_br_f_b, w_br_c_b = w_br_a.astype(BF16), w_br_f.astype(BF16), w_br_c.astype(BF16)
    w_out_b, w1_b, w2_b = w_out.astype(BF16), w_mlp1.astype(BF16), w_mlp2.astype(BF16)
    w_sp_b = w_spatial.astype(BF16)
    b_sp_t = b_spatial.transpose(0, 2, 1)
    g_sgu_row = g_sgu.reshape(depth, 1, w_c)
    gains = [g.reshape(depth, 1, d) for g in (g_pre_mix, g_post_mix, g_pre_mlp, g_post_mlp)]
    g_pre_mix_r, g_post_mix_r, g_pre_mlp_r, g_post_mlp_r = gains
    cache_k_r = cache_k.reshape(dec_batch, depth, past, w_a)
    cache_v_r = cache_v.reshape(dec_batch, depth, past, w_a)
    bias = _window_bias(rpb, dec_seq // GRID_W)

    col_f = 3 * w_a
    col_uv = col_f + w_f
    col_gates = col_uv + 2 * w_c

    x, h = _prologue(rows, x_prompt.reshape(m_ctx, d), x_sample.reshape(rows.m_lat, d),
                     g_pre_mix_r, mods, 0)
    new_k = new_v = None
    for l in range(depth):
        proj = _project(h, w_in_b, l, col_gates, 1536, F32)
        o_a_ctx, new_k, new_v = _ctx_attention(proj, new_k, new_v, l, depth, batch, seq, w_a)
        o_a_lat = _lat_attention(proj, cache_k_r, cache_v_r, bias, l, m_ctx, dec_batch, dec_seq, w_a)
        o_f_ctx = _fourier(proj, col_f, w_f, batch, seq, 0)
        o_f_lat = _fourier(proj, col_f, w_f, dec_batch, dec_seq, m_ctx // dec_seq)
        o_c = _spatial_gating(proj, col_uv, w_c, g_sgu_row, w_sp_b, b_sp_t, l)
        merged = _merge(rows, h, o_a_ctx, o_a_lat, o_f_ctx, o_f_lat, o_c, w_in_b, col_gates,
                        w_br_a_b, w_br_f_b, w_br_c_b, l)
        x, h2 = _out_projection(rows, merged, w_out_b, x, g_post_mix_r, g_pre_mlp_r, mods, l)
        x, h = _mlp(rows, h2, w1_b, w2_b, x, g_post_mlp_r, g_pre_mix_r, mods, l, l + 1 < depth)

    y_p = x[:m_ctx].reshape(batch, seq, d)
    y_s = x[m_ctx:].reshape(dec_batch, dec_seq, d)
    cache_shape = (batch, depth, seq, n_heads, HEAD_DIM)
    return (y_p, y_s, new_k.reshape(cache_shape), new_v.reshape(cache_shape))
```

```python
import functools
import math

import numpy as np
import jax
import jax.numpy as jnp
from jax import lax
from jax.experimental import pallas as pl
from jax.experimental.pallas import tpu as pltpu

F32 = jnp.float32
BF16 = jnp.bfloat16

RMS_EPS = 1e-6
MASKED_SCORE = -1e30

HEAD_DIM = 128
GROUP_DIM = 128
GRID_W = 64
CHUNK = 128
MAX_WIN_R = 8
WIN_C = 16
N_MOD = 6
MOD_SHIFT1, MOD_SCALE1, MOD_GATE1, MOD_SHIFT2, MOD_SCALE2, MOD_GATE2 = range(N_MOD)
MOD_ROWS = 8

MIB = 2 ** 20


def _params(semantics, vmem_mib):
    return pltpu.CompilerParams(dimension_semantics=semantics, vmem_limit_bytes=vmem_mib * MIB)


def _rms(x, g):
    return x * lax.rsqrt(jnp.mean(x * x, axis=-1, keepdims=True) + RMS_EPS) * g


def _dot(a, b):
    return jnp.dot(a, b, preferred_element_type=F32)


def _dot_nt(a, b):
    return lax.dot_general(a, b, (((1,), (1,)), ((), ())), preferred_element_type=F32)


def _ada_kernel(c_ref, w_ref, b_ref, o_ref):
    c = c_ref[...]
    s = c * jax.nn.sigmoid(c)
    o_ref[...] = _dot(s.astype(BF16), w_ref[...].astype(BF16)) + b_ref[...]


def _ada(cvec, w_ada, b_ada):
    depth, d, width = w_ada.shape
    tn = 1024
    return pl.pallas_call(
        _ada_kernel,
        name="ada",
        grid=(depth, width // tn),
        in_specs=[
            pl.BlockSpec((MOD_ROWS, d), lambda l, n: (0, 0)),
            pl.BlockSpec((None, d, tn), lambda l, n: (l, 0, n)),
            pl.BlockSpec((None, 1, tn), lambda l, n: (l, 0, n)),
        ],
        out_specs=pl.BlockSpec((None, MOD_ROWS, tn), lambda l, n: (l, 0, n)),
        out_shape=jax.ShapeDtypeStruct((depth, MOD_ROWS, width), F32),
        compiler_params=_params(("parallel", "parallel"), 40),
    )(cvec, w_ada, b_ada.reshape(depth, 1, width))


class _Rows:
    def __init__(self, m_ctx, dec_batch, dec_seq, d):
        self.m_ctx, self.dec_batch, self.dec_seq, self.d = m_ctx, dec_batch, dec_seq, d
        self.m_lat = dec_batch * dec_seq
        self.m = m_ctx + self.m_lat

    def group(self, i, tm):
        n_ctx = self.m_ctx // tm
        per_batch = self.dec_seq // tm
        return jnp.where(i < n_ctx, 0, 1 + (i - n_ctx) // per_batch)

    def mod_spec(self, layer, piece, tm, lag=0):
        return pl.BlockSpec((None, None, None, 1, self.d),
                            lambda i, *_: (layer, piece, self.group(jnp.maximum(i - lag, 0), tm), 0, 0))

    def gain_spec(self, layer):
        return pl.BlockSpec((None, 1, self.d), lambda i, *_: (layer, 0, 0))

    def split_specs(self, tm, width):
        n_ctx = self.m_ctx // tm
        ctx = pl.BlockSpec((tm, width), lambda i, *_: (jnp.minimum(i, n_ctx - 1), 0))
        lat = pl.BlockSpec((tm, width), lambda i, *_: (jnp.maximum(i - n_ctx, 0), 0))
        return ctx, lat

    def pick(self, tm, ctx_ref, lat_ref):
        return jnp.where(pl.program_id(0) < self.m_ctx // tm, ctx_ref[...], lat_ref[...])


def _prologue_kernel(xp_ref, xs_ref, g_ref, sc_ref, sh_ref, x_ref, h_ref, *, rows, tm):
    x = rows.pick(tm, xp_ref, xs_ref)
    x_ref[...] = x
    h = _rms(x, g_ref[...]) * (1.0 + sc_ref[...]) + sh_ref[...]
    h_ref[...] = h.astype(h_ref.dtype)


def _prologue(rows, x_ctx, x_lat, g_pre, mods, layer):
    tm = 512
    row_spec = pl.BlockSpec((tm, rows.d), lambda i: (i, 0))
    return pl.pallas_call(
        functools.partial(_prologue_kernel, rows=rows, tm=tm),
        name="prologue",
        grid=(rows.m // tm,),
        in_specs=[
            *rows.split_specs(tm, rows.d),
            rows.gain_spec(layer),
            rows.mod_spec(layer, MOD_SCALE1, tm),
            rows.mod_spec(layer, MOD_SHIFT1, tm),
        ],
        out_specs=[row_spec, row_spec],
        out_shape=[jax.ShapeDtypeStruct((rows.m, rows.d), F32),
                   jax.ShapeDtypeStruct((rows.m, rows.d), BF16)],
        compiler_params=_params(("arbitrary",), 40),
    )(x_ctx, x_lat, g_pre, mods, mods)


def _inproj_kernel(a_ref, w_ref, o_ref, wb_ref):
    @pl.when(pl.program_id(1) == 0)
    def _():
        wb_ref[...] = w_ref[...].astype(BF16)

    o_ref[...] = _dot(a_ref[...], wb_ref[...]).astype(o_ref.dtype)


def _project(h, w, layer, width, tn, out_dtype):
    m, k = h.shape
    tm = 1024
    assert width % tn == 0
    return pl.pallas_call(
        _inproj_kernel,
        name="in_proj",
        grid=(width // tn, m // tm),
        in_specs=[
            pl.BlockSpec((tm, k), lambda n, i: (i, 0)),
            pl.BlockSpec((None, k, tn), lambda n, i: (layer, 0, n)),
        ],
        out_specs=pl.BlockSpec((tm, tn), lambda n, i: (i, n)),
        out_shape=jax.ShapeDtypeStruct((m, width), out_dtype),
        scratch_shapes=[pltpu.VMEM((k, tn), BF16)],
        compiler_params=_params(("arbitrary", "arbitrary"), 58),
    )(h, w)


def _ctx_attn_kernel(q_ref, k_ref, v_ref, o_ref, ko_ref, vo_ref, *, n_heads, scale):
    ko_ref[...] = k_ref[...]
    vo_ref[...] = v_ref[...]
    for h in range(n_heads):
        cols = slice(h * HEAD_DIM, (h + 1) * HEAD_DIM)
        q = q_ref[:, cols].astype(BF16)
        k = k_ref[:, cols].astype(BF16)
        v = v_ref[:, cols].astype(BF16)
        s = _dot_nt(q, k) * scale
        p = jnp.exp(s - jnp.max(s, axis=-1, keepdims=True))
        denom = jnp.sum(p, axis=-1, keepdims=True)
        o_ref[:, cols] = (_dot(p.astype(BF16), v) / denom).astype(o_ref.dtype)


def _ctx_attention(proj, new_k, new_v, layer, depth, batch, seq, w_a):
    n_heads = w_a // HEAD_DIM
    kern = functools.partial(_ctx_attn_kernel, n_heads=n_heads, scale=1.0 / math.sqrt(HEAD_DIM))
    cache_spec = pl.BlockSpec((None, None, seq, w_a), lambda b: (b, layer, 0, 0))
    cache_shape = jax.ShapeDtypeStruct((batch, depth, seq, w_a), F32)
    in_specs = [pl.BlockSpec((seq, w_a), lambda b, j=j: (b, j)) for j in range(3)]
    args = [proj, proj, proj]
    aliases = {}
    if new_k is not None:
        in_specs += [pl.BlockSpec(memory_space=pl.ANY)] * 2
        args += [new_k, new_v]
        aliases = {3: 1, 4: 2}

    def body(q_ref, k_ref, v_ref, *rest):
        kern(q_ref, k_ref, v_ref, *rest[-3:])

    return pl.pallas_call(
        body,
        name="ctx_attn",
        grid=(batch,),
        in_specs=in_specs,
        out_specs=[pl.BlockSpec((seq, w_a), lambda b: (b, 0)), cache_spec, cache_spec],
        out_shape=[jax.ShapeDtypeStruct((batch * seq, w_a), BF16), cache_shape, cache_shape],
        input_output_aliases=aliases,
        compiler_params=_params(("arbitrary",), 32),
    )(*args)


def _window_start(r, rows):
    win_r = min(MAX_WIN_R, rows)
    return min(max(r - win_r // 2, 0), rows - win_r)


def _bias_variants(rows):
    variant_of, reps, seen = [], [], {}
    for r in range(rows):
        key = _window_start(r, rows) - r
        if key not in seen:
            seen[key] = len(reps)
            reps.append(r)
        variant_of.append(seen[key])
    return variant_of, reps


def _bias_kernel(rpb_ref, o_ref, *, rows):
    lh = pl.program_id(0)
    win_r = min(MAX_WIN_R, rows)
    n_dc = 2 * WIN_C - 1
    q = lax.broadcasted_iota(jnp.int32, (GRID_W, GRID_W), 0)
    kc = lax.broadcasted_iota(jnp.int32, (GRID_W, GRID_W), 1)
    dc = jnp.clip(kc - q, -(WIN_C - 1), WIN_C - 1) + (WIN_C - 1)
    c_start = jnp.clip(q - WIN_C // 2, 0, GRID_W - WIN_C)
    valid = (kc >= c_start) & (kc < c_start + WIN_C)
    tables = []
    for dr in range(2 * MAX_WIN_R - 1):
        t = jnp.zeros((GRID_W, GRID_W), F32)
        for d in range(n_dc):
            t = jnp.where(dc == d, rpb_ref[lh, dr * n_dc + d], t)
        tables.append(jnp.where(valid, t, MASKED_SCORE))
    _, reps = _bias_variants(rows)
    for var, r in enumerate(reps):
        for i in range(win_r):
            dr = _window_start(r, rows) + i - r + (MAX_WIN_R - 1)
            o_ref[var, :, i * GRID_W:(i + 1) * GRID_W] = tables[dr]


def _window_bias(rpb, rows):
    depth, n_heads, n_dr, n_dc = rpb.shape
    win_r = min(MAX_WIN_R, rows)
    n_var = len(_bias_variants(rows)[1])
    return pl.pallas_call(
        functools.partial(_bias_kernel, rows=rows),
        name="window_bias",
        grid=(depth * n_heads,),
        in_specs=[pl.BlockSpec(memory_space=pltpu.SMEM)],
        out_specs=pl.BlockSpec((None, n_var, GRID_W, win_r * GRID_W), lambda i: (i, 0, 0, 0)),
        out_shape=jax.ShapeDtypeStruct((depth * n_heads, n_var, GRID_W, win_r * GRID_W), F32),
        compiler_params=_params(("parallel",), 32),
    )(rpb.reshape(depth * n_heads, n_dr * n_dc))


def _lat_attn_kernel(q_ref, k_ref, v_ref, ck_ref, cv_ref, bias_ref, o_ref, *, rows, scale):
    win_r = min(MAX_WIN_R, rows)
    variant_of, _ = _bias_variants(rows)
    ck = ck_ref[...].astype(BF16)
    cv = cv_ref[...].astype(BF16)
    for r in range(rows):
        q = q_ref[r * GRID_W:(r + 1) * GRID_W, :].astype(BF16)
        w0 = _window_start(r, rows) * GRID_W
        kw = k_ref[w0:w0 + win_r * GRID_W, :].astype(BF16)
        vw = v_ref[w0:w0 + win_r * GRID_W, :].astype(BF16)
        s_lat = _dot_nt(q, kw) * scale + bias_ref[variant_of[r]]
        s_ctx = _dot_nt(q, ck) * scale
        top = jnp.maximum(jnp.max(s_lat, axis=-1, keepdims=True),
                          jnp.max(s_ctx, axis=-1, keepdims=True))
        p_lat = jnp.exp(s_lat - top)
        p_ctx = jnp.exp(s_ctx - top)
        denom = jnp.sum(p_lat, axis=-1, keepdims=True) + jnp.sum(p_ctx, axis=-1, keepdims=True)
        o = _dot(p_lat.astype(BF16), vw) + _dot(p_ctx.astype(BF16), cv)
        o_ref[r * GRID_W:(r + 1) * GRID_W, :] = (o / denom).astype(o_ref.dtype)


def _lat_attention(proj, cache_k, cache_v, bias, layer, m_ctx, dec_batch, dec_seq, w_a):
    n_heads = w_a // HEAD_DIM
    rows = dec_seq // GRID_W
    past = cache_k.shape[2]
    first = m_ctx // dec_seq
    assert first * dec_seq == m_ctx
    kern = functools.partial(_lat_attn_kernel, rows=rows, scale=1.0 / math.sqrt(HEAD_DIM))
    qkv_specs = [pl.BlockSpec((dec_seq, HEAD_DIM), lambda b, h, j=j: (first + b, j * n_heads + h))
                 for j in range(3)]
    cache_spec = pl.BlockSpec((None, None, past, HEAD_DIM), lambda b, h: (b, layer, 0, h))
    n_var = bias.shape[1]
    return pl.pallas_call(
        kern,
        name="lat_attn",
        grid=(dec_batch, n_heads),
        in_specs=qkv_specs + [
            cache_spec, cache_spec,
            pl.BlockSpec((None, n_var) + bias.shape[2:], lambda b, h: (layer * n_heads + h, 0, 0, 0)),
        ],
        out_specs=pl.BlockSpec((dec_seq, HEAD_DIM), lambda b, h: (b, h)),
        out_shape=jax.ShapeDtypeStruct((dec_batch * dec_seq, w_a), BF16),
        compiler_params=_params(("parallel", "parallel"), 32),
    )(proj, proj, proj, cache_k, cache_v, bias)


def _dft_tables(n):
    idx = np.arange(n, dtype=np.int64)
    ang = 2.0 * np.pi * ((idx[:, None] * idx[None, :]) % n) / n
    return np.cos(ang), np.sin(ang)


def _np_split_bf16(a):
    a32 = np.asarray(a, np.float32)
    hi = a32.astype(BF16)
    lo = (a32 - hi.astype(np.float32)).astype(BF16)
    return jnp.asarray(hi), jnp.asarray(lo)


def _fourier_kernel(f_ref, wc_hi_ref, wc_lo_ref, cn_hi_ref, cn_lo_ref, sn_hi_ref, sn_lo_ref,
                    o_ref, *, n_groups, scale):
    wc_hi, wc_lo = wc_hi_ref[...], wc_lo_ref[...]
    pc, ps = [], []
    for g in range(n_groups):
        x = f_ref[:, g * GROUP_DIM:(g + 1) * GROUP_DIM].astype(BF16)
        p = _dot(x, wc_hi) + _dot(x, wc_lo)
        pc.append(p[:, :GROUP_DIM])
        ps.append(p[:, GROUP_DIM:])

    def left2(m_hi_ref, m_lo_ref, b):
        b = b.astype(BF16)
        return _dot(m_hi_ref[...], b) + _dot(m_lo_ref[...], b)

    y = (left2(cn_hi_ref, cn_lo_ref, jnp.concatenate(pc, axis=1))
         - left2(sn_hi_ref, sn_lo_ref, jnp.concatenate(ps, axis=1)))
    o_ref[...] = (y * scale).astype(o_ref.dtype)


def _fourier(proj, col_start, w_f, n_batch, n_pos, first_block):
    col_block = col_start // w_f
    assert col_block * w_f == col_start
    cc, sc = _dft_tables(GROUP_DIM)
    wc_hi, wc_lo = _np_split_bf16(np.concatenate([cc, sc], axis=1))
    cn, sn = _dft_tables(n_pos)
    cn_hi, cn_lo = _np_split_bf16(cn)
    sn_hi, sn_lo = _np_split_bf16(sn)
    const = lambda a: pl.BlockSpec(a.shape, lambda b: (0, 0))
    kern = functools.partial(_fourier_kernel, n_groups=w_f // GROUP_DIM,
                             scale=1.0 / math.sqrt(n_pos * GROUP_DIM))
    tables = (wc_hi, wc_lo, cn_hi, cn_lo, sn_hi, sn_lo)
    return pl.pallas_call(
        kern,
        name="fourier",
        grid=(n_batch,),
        in_specs=[pl.BlockSpec((n_pos, w_f), lambda b: (first_block + b, col_block))]
                 + [const(t) for t in tables],
        out_specs=pl.BlockSpec((n_pos, w_f), lambda b: (b, 0)),
        out_shape=jax.ShapeDtypeStruct((n_batch * n_pos, w_f), BF16),
        compiler_params=_params(("parallel",), 48),
    )(proj, *tables)


def _gelu(x):
    return 0.5 * x * (1.0 + lax.erf(x * math.sqrt(0.5)))


def _sgu_kernel(u_ref, v_ref, g_ref, w_ref, b_ref, o_ref, *, n_groups, n_chunks):
    for g in range(n_groups):
        cols = slice(g * GROUP_DIM, (g + 1) * GROUP_DIM)
        v = _rms(_gelu(v_ref[:, cols]), g_ref[:, cols]).astype(BF16)
        v_wide = jnp.concatenate([v[c * CHUNK:(c + 1) * CHUNK, :] for c in range(n_chunks)], axis=1)
        s = _dot(w_ref[g], v_wide) + b_ref[:, g:g + 1]
        for c in range(n_chunks):
            rws = slice(c * CHUNK, (c + 1) * CHUNK)
            o_ref[rws, cols] = (_gelu(u_ref[rws, cols])
                                * s[:, c * GROUP_DIM:(c + 1) * GROUP_DIM]).astype(o_ref.dtype)


def _spatial_gating(proj, col_start, w_c, g_sgu, w_sp, b_sp_t, layer):
    m = proj.shape[0]
    n_groups = w_c // GROUP_DIM
    n_chunks = 4
    tm = n_chunks * CHUNK
    col_block = col_start // w_c
    assert col_block * w_c == col_start
    kern = functools.partial(_sgu_kernel, n_groups=n_groups, n_chunks=n_chunks)
    return pl.pallas_call(
        kern,
        name="spatial_gate",
        grid=(m // tm,),
        in_specs=[
            pl.BlockSpec((tm, w_c), lambda i: (i, col_block)),
            pl.BlockSpec((tm, w_c), lambda i: (i, col_block + 1)),
            pl.BlockSpec((None, 1, w_c), lambda i: (layer, 0, 0)),
            pl.BlockSpec((None, n_groups, CHUNK, CHUNK), lambda i: (layer, 0, 0, 0)),
            pl.BlockSpec((None, CHUNK, n_groups), lambda i: (layer, 0, 0)),
        ],
        out_specs=pl.BlockSpec((tm, w_c), lambda i: (i, 0)),
        out_shape=jax.ShapeDtypeStruct((m, w_c), BF16),
        compiler_params=_params(("parallel",), 32),
    )(proj, proj, g_sgu, w_sp, b_sp_t)


def _merge_kernel(h_ref, oa_ctx_ref, oa_lat_ref, of_ctx_ref, of_lat_ref, oc_ref,
                  wga_ref, wgf_ref, wgc_ref, wa_ref, wf_ref, wc_ref, o_ref, *, rows, tm):
    h = h_ref[...]
    o_a = rows.pick(tm, oa_ctx_ref, oa_lat_ref)
    o_f = rows.pick(tm, of_ctx_ref, of_lat_ref)
    acc = jax.nn.sigmoid(_dot(h, wga_ref[...])) * _dot(o_a, wa_ref[...])
    acc += jax.nn.sigmoid(_dot(h, wgf_ref[...])) * _dot(o_f, wf_ref[...])
    acc += jax.nn.sigmoid(_dot(h, wgc_ref[...])) * _dot(oc_ref[...], wc_ref[...])
    o_ref[...] = acc.astype(o_ref.dtype)


def _merge(rows, h, o_a_ctx, o_a_lat, o_f_ctx, o_f_lat, o_c, w_in, col_gates,
           w_br_a, w_br_f, w_br_c, layer):
    d = rows.d
    tm, tn = 1024, 512
    nb = d // tn
    gate_block = col_gates // tn
    assert gate_block * tn == col_gates
    full = lambda width: pl.BlockSpec((tm, width), lambda i, n: (i, 0))
    gate_w = lambda j: pl.BlockSpec((None, d, tn), lambda i, n: (layer, 0, gate_block + j * nb + n))
    br_w = lambda w: pl.BlockSpec((None, w.shape[1], tn), lambda i, n: (layer, 0, n))
    return pl.pallas_call(
        functools.partial(_merge_kernel, rows=rows, tm=tm),
        name="gate_merge",
        grid=(rows.m // tm, nb),
        in_specs=[full(d),
                  *rows.split_specs(tm, o_a_ctx.shape[1]),
                  *rows.split_specs(tm, o_f_ctx.shape[1]),
                  full(o_c.shape[1]),
                  gate_w(0), gate_w(1), gate_w(2),
                  br_w(w_br_a), br_w(w_br_f), br_w(w_br_c)],
        out_specs=pl.BlockSpec((tm, tn), lambda i, n: (i, n)),
        out_shape=jax.ShapeDtypeStruct((rows.m, d), BF16),
        compiler_params=_params(("arbitrary", "arbitrary"), 56),
    )(h, o_a_ctx, o_a_lat, o_f_ctx, o_f_lat, o_c, w_in, w_in, w_in, w_br_a, w_br_f, w_br_c)


SUB_ROWS = 256


def _unit_rms(y):
    return y * lax.rsqrt(jnp.mean(y * y, axis=-1, keepdims=True) + RMS_EPS)


def _outproj_kernel(mg_ref, w_ref, x_ref, gpost_ref, gt_ref, gpre_ref, sc_ref, sh_ref,
                    xo_ref, ho_ref, y_even_ref, y_odd_ref, *, n_tiles):
    s = pl.program_id(0)
    y_refs = (y_even_ref, y_odd_ref)

    def multiply(parity):
        y_refs[parity][...] = _dot(mg_ref[...], w_ref[...])

    def finish(parity):
        post = gpost_ref[...] * gt_ref[...]
        pre = gpre_ref[...] * (1.0 + sc_ref[...])
        x1 = x_ref[...] + _unit_rms(y_refs[parity][...]) * post
        xo_ref[...] = x1
        ho_ref[...] = (_unit_rms(x1) * pre + sh_ref[...]).astype(ho_ref.dtype)

    @pl.when(s == 0)
    def _():
        multiply(0)

    for parity in (0, 1):
        @pl.when((s > 0) & (s < n_tiles) & (s % 2 == parity))
        def _():
            multiply(parity)
            finish(1 - parity)

    @pl.when(s == n_tiles)
    def _():
        finish((n_tiles - 1) % 2)


def _out_projection(rows, merged, w_out, x, g_post_mix, g_pre_mlp, mods, layer):
    tm = 512
    d = rows.d
    n_tiles = rows.m // tm
    ahead_spec = pl.BlockSpec((tm, d), lambda s: (jnp.minimum(s, n_tiles - 1), 0))
    lag_spec = pl.BlockSpec((tm, d), lambda s: (jnp.maximum(s - 1, 0), 0))
    return pl.pallas_call(
        functools.partial(_outproj_kernel, n_tiles=n_tiles),
        name="out_proj",
        grid=(n_tiles + 1,),
        in_specs=[
            ahead_spec,
            pl.BlockSpec((None, d, d), lambda s: (layer, 0, 0), pipeline_mode=pl.Buffered(1)),
            lag_spec,
            rows.gain_spec(layer),
            rows.mod_spec(layer, MOD_GATE1, tm, lag=1),
            rows.gain_spec(layer),
            rows.mod_spec(layer, MOD_SCALE2, tm, lag=1),
            rows.mod_spec(layer, MOD_SHIFT2, tm, lag=1),
        ],
        out_specs=[lag_spec, lag_spec],
        out_shape=[jax.ShapeDtypeStruct((rows.m, d), F32), jax.ShapeDtypeStruct((rows.m, d), BF16)],
        scratch_shapes=[pltpu.VMEM((tm, d), F32), pltpu.VMEM((tm, d), F32)],
        compiler_params=_params(("arbitrary",), 52),
    )(merged, w_out, x, g_post_mix, mods, g_pre_mlp, mods, mods)


def _mlp_kernel(*refs, n_ff_steps, has_next, rows, tm):
    if has_next:
        (h_ref, w1_ref, w2_ref, x_ref, gpost_ref, gt_ref, gpre_ref, sc_ref, sh_ref,
         xo_ref, ho_ref, acc_ref) = refs
    else:
        h_ref, w1_ref, w2_ref, x_ref, gpost_ref, gt_ref, yp_ref, ys_ref, acc_ref = refs
    step = pl.program_id(1)
    last = n_ff_steps - 1
    assert last >= 2

    def ff_slice(rws):
        hid = jnp.square(jnp.maximum(_dot(h_ref[rws, :], w1_ref[...]), 0.0)).astype(BF16)
        return _dot(hid, w2_ref[...])

    whole = slice(None)

    @pl.when(step == 0)
    def _():
        acc_ref[...] = ff_slice(whole)

    @pl.when((step > 0) & (step < last))
    def _():
        acc_ref[...] += ff_slice(whole)

    @pl.when(step == last)
    def _():
        post = gpost_ref[...] * gt_ref[...]
        if has_next:
            pre = gpre_ref[...] * (1.0 + sc_ref[...])
            shift = sh_ref[...]
        for c in range(tm // SUB_ROWS):
            rws = slice(c * SUB_ROWS, (c + 1) * SUB_ROWS)
            x2 = x_ref[rws, :] + _unit_rms(acc_ref[rws, :] + ff_slice(rws)) * post
            if has_next:
                xo_ref[rws, :] = x2
                ho_ref[rws, :] = (_unit_rms(x2) * pre + shift).astype(ho_ref.dtype)
            else:
                acc_ref[rws, :] = x2
        if not has_next:
            is_ctx = pl.program_id(0) < rows.m_ctx // tm

            @pl.when(is_ctx)
            def _():
                yp_ref[...] = acc_ref[...]

            @pl.when(jnp.logical_not(is_ctx))
            def _():
                ys_ref[...] = acc_ref[...]


def _mlp(rows, h2, w1, w2, x, g_post_mlp, g_pre_next, mods, layer, has_next):
    tm, tf = 512, 1024
    d = rows.d
    d_ff = w1.shape[2]
    n_ff_steps = d_ff // tf
    row_spec = pl.BlockSpec((tm, d), lambda i, f: (i, 0))
    in_specs = [
        row_spec,
        pl.BlockSpec((None, d, tf), lambda i, f: (layer, 0, f)),
        pl.BlockSpec((None, tf, d), lambda i, f: (layer, f, 0)),
        row_spec,
        rows.gain_spec(layer),
        rows.mod_spec(layer, MOD_GATE2, tm),
    ]
    args = [h2, w1, w2, x, g_post_mlp, mods]
    if has_next:
        in_specs += [rows.gain_spec(layer + 1),
                     rows.mod_spec(layer + 1, MOD_SCALE1, tm),
                     rows.mod_spec(layer + 1, MOD_SHIFT1, tm)]
        args += [g_pre_next, mods, mods]
        out_specs = [row_spec, row_spec]
        out_shape = [jax.ShapeDtypeStruct((rows.m, d), F32),
                     jax.ShapeDtypeStruct((rows.m, d), BF16)]
    else:
        out_specs = list(rows.split_specs(tm, d))
        out_shape = [jax.ShapeDtypeStruct((rows.m_ctx, d), F32),
                     jax.ShapeDtypeStruct((rows.m_lat, d), F32)]
    return pl.pallas_call(
        functools.partial(_mlp_kernel, n_ff_steps=n_ff_steps, has_next=has_next, rows=rows, tm=tm),
        name="mlp",
        grid=(rows.m // tm, n_ff_steps),
        in_specs=in_specs,
        out_specs=out_specs,
        out_shape=out_shape,
        scratch_shapes=[pltpu.VMEM((tm, d), F32)],
        compiler_params=_params(("arbitrary", "arbitrary"), 56),
    )(*args)


def kernel(x_prompt, x_sample, cache_k, cache_v, c, c_ctx, w_ada, b_ada, g_pre_mix, g_post_mix,
           g_pre_mlp, g_post_mlp, w_in, rpb, g_sgu, w_spatial, b_spatial, w_br_a, w_br_f, w_br_c,
           w_out, w_mlp1, w_mlp2):
    batch, seq, d = x_prompt.shape
    dec_batch, dec_seq, _ = x_sample.shape
    depth = w_in.shape[0]
    past = cache_k.shape[2]
    w_a = w_br_a.shape[1]
    w_f = w_br_f.shape[1]
    w_c = w_br_c.shape[1]
    n_heads = w_a // HEAD_DIM
    m_ctx = batch * seq
    rows = _Rows(m_ctx, dec_batch, dec_seq, d)
    assert dec_seq % GRID_W == 0 and m_ctx % dec_seq == 0 and MOD_ROWS >= 1 + dec_batch

    cvec = jnp.zeros((MOD_ROWS, d), F32).at[0].set(c_ctx).at[1:1 + dec_batch].set(c)
    mods = _ada(cvec, w_ada, b_ada)
    mods = mods[:, :1 + dec_batch].reshape(depth, 1 + dec_batch, N_MOD, d)
    mods = mods.transpose(0, 2, 1, 3)[:, :, :, None, :]

    col_f = 3 * w_a
    col_uv = col_f + w_f
    col_gates = col_uv + 2 * w_c

    w_gates_b = w_in[:, :, col_gates:].astype(BF16)
    w_br_a_b, w_br_f_b, w_br_c_b = w_br_a.astype(BF16), w_br_f.astype(BF16), w_br_c.astype(BF16)
    w_out_b, w1_b, w2_b = w_out.astype(BF16), w_mlp1.astype(BF16), w_mlp2.astype(BF16)
    w_sp_b = w_spatial.astype(BF16)
    b_sp_t = b_spatial.transpose(0, 2, 1)
    g_sgu_row = g_sgu.reshape(depth, 1, w_c)
    gains = [g.reshape(depth, 1, d) for g in (g_pre_mix, g_post_mix, g_pre_mlp, g_post_mlp)]
    g_pre_mix_r, g_post_mix_r, g_pre_mlp_r, g_post_mlp_r = gains
    cache_k_r = cache_k.reshape(dec_batch, depth, past, w_a)
    cache_v_r = cache_v.reshape(dec_batch, depth, past, w_a)
    bias = _window_bias(rpb, dec_seq // GRID_W)

    x, h = _prologue(rows, x_prompt.reshape(m_ctx, d), x_sample.reshape(rows.m_lat, d),
                     g_pre_mix_r, mods, 0)
    new_k = new_v = None
    for l in range(depth):
        proj = _project(h, w_in, l, col_gates, 1536, F32)
        o_a_ctx, new_k, new_v = _ctx_attention(proj, new_k, new_v, l, depth, batch, seq, w_a)
        o_a_lat = _lat_attention(proj, cache_k_r, cache_v_r, bias, l, m_ctx, dec_batch, dec_seq, w_a)
        o_f_ctx = _fourier(proj, col_f, w_f, batch, seq, 0)
        o_f_lat = _fourier(proj, col_f, w_f, dec_batch, dec_seq, m_ctx // dec_seq)
        o_c = _spatial_gating(proj, col_uv, w_c, g_sgu_row, w_sp_b, b_sp_t, l)
        merged = _merge(rows, h, o_a_ctx, o_a_lat, o_f_ctx, o_f_lat, o_c, w_gates_b, 0,
                        w_br_a_b, w_br_f_b, w_br_c_b, l)
        x, h2 = _out_projection(rows, merged, w_out_b, x, g_post_mix_r, g_pre_mlp_r, mods, l)
        x, h = _mlp(rows, h2, w1_b, w2_b, x, g_post_mlp_r, g_pre_mix_r, mods, l, l + 1 < depth)

    y_p = x.reshape(batch, seq, d)
    y_s = h.reshape(dec_batch, dec_seq, d)
    cache_shape = (batch, depth, seq, n_heads, HEAD_DIM)
    return (y_p, y_s, new_k.reshape(cache_shape), new_v.reshape(cache_shape))
```

```python
import functools
import math

import numpy as np
import jax
import jax.numpy as jnp
from jax import lax
from jax.experimental import pallas as pl
from jax.experimental.pallas import tpu as pltpu

F32 = jnp.float32
BF16 = jnp.bfloat16

RMS_EPS = 1e-6
MASKED_SCORE = -1e30

HEAD_DIM = 128
GROUP_DIM = 128
GRID_W = 64
CHUNK = 128
MAX_WIN_R = 8
WIN_C = 16
N_MOD = 6
MOD_SHIFT1, MOD_SCALE1, MOD_GATE1, MOD_SHIFT2, MOD_SCALE2, MOD_GATE2 = range(N_MOD)
MOD_ROWS = 8

MIB = 2 ** 20


def _params(semantics, vmem_mib):
    return pltpu.CompilerParams(dimension_semantics=semantics, vmem_limit_bytes=vmem_mib * MIB)


def _rms(x, g):
    return x * lax.rsqrt(jnp.mean(x * x, axis=-1, keepdims=True) + RMS_EPS) * g


def _dot(a, b):
    return jnp.dot(a, b, preferred_element_type=F32)


def _dot_nt(a, b):
    return lax.dot_general(a, b, (((1,), (1,)), ((), ())), preferred_element_type=F32)


def _ada_kernel(c_ref, w_ref, b_ref, o_ref):
    c = c_ref[...]
    s = c * jax.nn.sigmoid(c)
    o_ref[...] = _dot(s.astype(BF16), w_ref[...].astype(BF16)) + b_ref[...]


def _ada(cvec, w_ada, b_ada):
    depth, d, width = w_ada.shape
    tn = 1024
    return pl.pallas_call(
        _ada_kernel,
        name="ada",
        grid=(depth, width // tn),
        in_specs=[
            pl.BlockSpec((MOD_ROWS, d), lambda l, n: (0, 0)),
            pl.BlockSpec((None, d, tn), lambda l, n: (l, 0, n)),
            pl.BlockSpec((None, 1, tn), lambda l, n: (l, 0, n)),
        ],
        out_specs=pl.BlockSpec((None, MOD_ROWS, tn), lambda l, n: (l, 0, n)),
        out_shape=jax.ShapeDtypeStruct((depth, MOD_ROWS, width), F32),
        compiler_params=_params(("parallel", "parallel"), 40),
    )(cvec, w_ada, b_ada.reshape(depth, 1, width))


class _Rows:
    def __init__(self, m_ctx, dec_batch, dec_seq, d):
        self.m_ctx, self.dec_batch, self.dec_seq, self.d = m_ctx, dec_batch, dec_seq, d
        self.m_lat = dec_batch * dec_seq
        self.m = m_ctx + self.m_lat

    def group(self, i, tm):
        n_ctx = self.m_ctx // tm
        per_batch = self.dec_seq // tm
        return jnp.where(i < n_ctx, 0, 1 + (i - n_ctx) // per_batch)

    def mod_spec(self, layer, piece, tm, lag=0):
        return pl.BlockSpec((None, None, None, 1, self.d),
                            lambda i, *_: (layer, piece, self.group(jnp.maximum(i - lag, 0), tm), 0, 0))

    def gain_spec(self, layer):
        return pl.BlockSpec((None, 1, self.d), lambda i, *_: (layer, 0, 0))

    def split_specs(self, tm, width):
        n_ctx = self.m_ctx // tm
        ctx = pl.BlockSpec((tm, width), lambda i, *_: (jnp.minimum(i, n_ctx - 1), 0))
        lat = pl.BlockSpec((tm, width), lambda i, *_: (jnp.maximum(i - n_ctx, 0), 0))
        return ctx, lat

    def pick(self, tm, ctx_ref, lat_ref):
        return jnp.where(pl.program_id(0) < self.m_ctx // tm, ctx_ref[...], lat_ref[...])


def _prologue_kernel(xp_ref, xs_ref, g_ref, sc_ref, sh_ref, x_ref, h_ref, *, rows, tm):
    x = rows.pick(tm, xp_ref, xs_ref)
    x_ref[...] = x
    h = _rms(x, g_ref[...]) * (1.0 + sc_ref[...]) + sh_ref[...]
    h_ref[...] = h.astype(h_ref.dtype)


def _prologue(rows, x_ctx, x_lat, g_pre, mods, layer):
    tm = 512
    row_spec = pl.BlockSpec((tm, rows.d), lambda i: (i, 0))
    return pl.pallas_call(
        functools.partial(_prologue_kernel, rows=rows, tm=tm),
        name="prologue",
        grid=(rows.m // tm,),
        in_specs=[
            *rows.split_specs(tm, rows.d),
            rows.gain_spec(layer),
            rows.mod_spec(layer, MOD_SCALE1, tm),
            rows.mod_spec(layer, MOD_SHIFT1, tm),
        ],
        out_specs=[row_spec, row_spec],
        out_shape=[jax.ShapeDtypeStruct((rows.m, rows.d), F32),
                   jax.ShapeDtypeStruct((rows.m, rows.d), BF16)],
        compiler_params=_params(("arbitrary",), 40),
    )(x_ctx, x_lat, g_pre, mods, mods)


def _inproj_kernel(a_ref, w_ref, o_ref, wb_ref):
    @pl.when(pl.program_id(1) == 0)
    def _():
        wb_ref[...] = w_ref[...].astype(BF16)

    o_ref[...] = _dot(a_ref[...], wb_ref[...]).astype(o_ref.dtype)


def _project(h, w, layer, width, tn, out_dtype):
    m, k = h.shape
    tm = 1024
    assert width % tn == 0
    return pl.pallas_call(
        _inproj_kernel,
        name="in_proj",
        grid=(width // tn, m // tm),
        in_specs=[
            pl.BlockSpec((tm, k), lambda n, i: (i, 0)),
            pl.BlockSpec((None, k, tn), lambda n, i: (layer, 0, n)),
        ],
        out_specs=pl.BlockSpec((tm, tn), lambda n, i: (i, n)),
        out_shape=jax.ShapeDtypeStruct((m, width), out_dtype),
        scratch_shapes=[pltpu.VMEM((k, tn), BF16)],
        compiler_params=_params(("arbitrary", "arbitrary"), 58),
    )(h, w)


def _ctx_attn_kernel(q_ref, k_ref, v_ref, o_ref, ko_ref, vo_ref, *, n_heads, scale):
    ko_ref[...] = k_ref[...]
    vo_ref[...] = v_ref[...]
    for h in range(n_heads):
        cols = slice(h * HEAD_DIM, (h + 1) * HEAD_DIM)
        q = q_ref[:, cols].astype(BF16)
        k = k_ref[:, cols].astype(BF16)
        v = v_ref[:, cols].astype(BF16)
        s = _dot_nt(q, k) * scale
        p = jnp.exp(s - jnp.max(s, axis=-1, keepdims=True))
        denom = jnp.sum(p, axis=-1, keepdims=True)
        o_ref[:, cols] = (_dot(p.astype(BF16), v) / denom).astype(o_ref.dtype)


def _ctx_attention(proj, new_k, new_v, layer, depth, batch, seq, w_a):
    n_heads = w_a // HEAD_DIM
    kern = functools.partial(_ctx_attn_kernel, n_heads=n_heads, scale=1.0 / math.sqrt(HEAD_DIM))
    cache_spec = pl.BlockSpec((None, None, seq, w_a), lambda b: (b, layer, 0, 0))
    cache_shape = jax.ShapeDtypeStruct((batch, depth, seq, w_a), F32)
    in_specs = [pl.BlockSpec((seq, w_a), lambda b, j=j: (b, j)) for j in range(3)]
    in_specs += [pl.BlockSpec(memory_space=pl.ANY)] * 2

    def body(q_ref, k_ref, v_ref, k_all_ref, v_all_ref, o_ref, ko_ref, vo_ref):
        del k_all_ref, v_all_ref
        kern(q_ref, k_ref, v_ref, o_ref, ko_ref, vo_ref)

    return pl.pallas_call(
        body,
        name="ctx_attn",
        grid=(batch,),
        in_specs=in_specs,
        out_specs=[pl.BlockSpec((seq, w_a), lambda b: (b, 0)), cache_spec, cache_spec],
        out_shape=[jax.ShapeDtypeStruct((batch * seq, w_a), BF16), cache_shape, cache_shape],
        input_output_aliases={3: 1, 4: 2},
        compiler_params=_params(("arbitrary",), 32),
    )(proj, proj, proj, new_k, new_v)


def _window_start(r, rows):
    win_r = min(MAX_WIN_R, rows)
    return min(max(r - win_r // 2, 0), rows - win_r)


def _bias_variants(rows):
    variant_of, reps, seen = [], [], {}
    for r in range(rows):
        key = _window_start(r, rows) - r
        if key not in seen:
            seen[key] = len(reps)
            reps.append(r)
        variant_of.append(seen[key])
    return variant_of, reps


def _bias_kernel(rpb_ref, o_ref, *, rows):
    lh = pl.program_id(0)
    win_r = min(MAX_WIN_R, rows)
    n_dc = 2 * WIN_C - 1
    q = lax.broadcasted_iota(jnp.int32, (GRID_W, GRID_W), 0)
    kc = lax.broadcasted_iota(jnp.int32, (GRID_W, GRID_W), 1)
    dc = jnp.clip(kc - q, -(WIN_C - 1), WIN_C - 1) + (WIN_C - 1)
    c_start = jnp.clip(q - WIN_C // 2, 0, GRID_W - WIN_C)
    valid = (kc >= c_start) & (kc < c_start + WIN_C)
    tables = []
    for dr in range(2 * MAX_WIN_R - 1):
        t = jnp.zeros((GRID_W, GRID_W), F32)
        for d in range(n_dc):
            t = jnp.where(dc == d, rpb_ref[lh, dr * n_dc + d], t)
        tables.append(jnp.where(valid, t, MASKED_SCORE))
    _, reps = _bias_variants(rows)
    for var, r in enumerate(reps):
        for i in range(win_r):
            dr = _window_start(r, rows) + i - r + (MAX_WIN_R - 1)
            o_ref[var, :, i * GRID_W:(i + 1) * GRID_W] = tables[dr]


def _window_bias(rpb, rows):
    depth, n_heads, n_dr, n_dc = rpb.shape
    win_r = min(MAX_WIN_R, rows)
    n_var = len(_bias_variants(rows)[1])
    return pl.pallas_call(
        functools.partial(_bias_kernel, rows=rows),
        name="window_bias",
        grid=(depth * n_heads,),
        in_specs=[pl.BlockSpec(memory_space=pltpu.SMEM)],
        out_specs=pl.BlockSpec((None, n_var, GRID_W, win_r * GRID_W), lambda i: (i, 0, 0, 0)),
        out_shape=jax.ShapeDtypeStruct((depth * n_heads, n_var, GRID_W, win_r * GRID_W), F32),
        compiler_params=_params(("parallel",), 32),
    )(rpb.reshape(depth * n_heads, n_dr * n_dc))


def _lat_attn_kernel(q_ref, k_ref, v_ref, ck_ref, cv_ref, bias_ref, o_ref, *, rows, scale):
    win_r = min(MAX_WIN_R, rows)
    variant_of, _ = _bias_variants(rows)
    ck = ck_ref[...].astype(BF16)
    cv = cv_ref[...].astype(BF16)
    for r in range(rows):
        q = q_ref[r * GRID_W:(r + 1) * GRID_W, :].astype(BF16)
        w0 = _window_start(r, rows) * GRID_W
        kw = k_ref[w0:w0 + win_r * GRID_W, :].astype(BF16)
        vw = v_ref[w0:w0 + win_r * GRID_W, :].astype(BF16)
        s_lat = _dot_nt(q, kw) * scale + bias_ref[variant_of[r]]
        s_ctx = _dot_nt(q, ck) * scale
        top = jnp.maximum(jnp.max(s_lat, axis=-1, keepdims=True),
                          jnp.max(s_ctx, axis=-1, keepdims=True))
        p_lat = jnp.exp(s_lat - top)
        p_ctx = jnp.exp(s_ctx - top)
        denom = jnp.sum(p_lat, axis=-1, keepdims=True) + jnp.sum(p_ctx, axis=-1, keepdims=True)
        o = _dot(p_lat.astype(BF16), vw) + _dot(p_ctx.astype(BF16), cv)
        o_ref[r * GRID_W:(r + 1) * GRID_W, :] = (o / denom).astype(o_ref.dtype)


def _lat_attention(proj, cache_k, cache_v, bias, layer, m_ctx, dec_batch, dec_seq, w_a):
    n_heads = w_a // HEAD_DIM
    rows = dec_seq // GRID_W
    past = cache_k.shape[2]
    first = m_ctx // dec_seq
    assert first * dec_seq == m_ctx
    kern = functools.partial(_lat_attn_kernel, rows=rows, scale=1.0 / math.sqrt(HEAD_DIM))
    qkv_specs = [pl.BlockSpec((dec_seq, HEAD_DIM), lambda b, h, j=j: (first + b, j * n_heads + h))
                 for j in range(3)]
    cache_spec = pl.BlockSpec((None, None, past, HEAD_DIM), lambda b, h: (b, layer, 0, h))
    n_var = bias.shape[1]
    return pl.pallas_call(
        kern,
        name="lat_attn",
        grid=(dec_batch, n_heads),
        in_specs=qkv_specs + [
            cache_spec, cache_spec,
            pl.BlockSpec((None, n_var) + bias.shape[2:], lambda b, h: (layer * n_heads + h, 0, 0, 0)),
        ],
        out_specs=pl.BlockSpec((dec_seq, HEAD_DIM), lambda b, h: (b, h)),
        out_shape=jax.ShapeDtypeStruct((dec_batch * dec_seq, w_a), BF16),
        compiler_params=_params(("parallel", "parallel"), 32),
    )(proj, proj, proj, cache_k, cache_v, bias)


def _dft_tables(n):
    idx = np.arange(n, dtype=np.int64)
    ang = 2.0 * np.pi * ((idx[:, None] * idx[None, :]) % n) / n
    return np.cos(ang), np.sin(ang)


def _np_split_bf16(a):
    a32 = np.asarray(a, np.float32)
    hi = a32.astype(BF16)
    lo = (a32 - hi.astype(np.float32)).astype(BF16)
    return jnp.asarray(hi), jnp.asarray(lo)


def _fourier_kernel(f_ref, wc_hi_ref, wc_lo_ref, cn_hi_ref, cn_lo_ref, sn_hi_ref, sn_lo_ref,
                    o_ref, *, n_groups, scale):
    wc_hi, wc_lo = wc_hi_ref[...], wc_lo_ref[...]
    pc, ps = [], []
    for g in range(n_groups):
        x = f_ref[:, g * GROUP_DIM:(g + 1) * GROUP_DIM].astype(BF16)
        p = _dot(x, wc_hi) + _dot(x, wc_lo)
        pc.append(p[:, :GROUP_DIM])
        ps.append(p[:, GROUP_DIM:])

    def left2(m_hi_ref, m_lo_ref, b):
        b = b.astype(BF16)
        return _dot(m_hi_ref[...], b) + _dot(m_lo_ref[...], b)

    y = (left2(cn_hi_ref, cn_lo_ref, jnp.concatenate(pc, axis=1))
         - left2(sn_hi_ref, sn_lo_ref, jnp.concatenate(ps, axis=1)))
    o_ref[...] = (y * scale).astype(o_ref.dtype)


def _fourier(proj, col_start, w_f, n_batch, n_pos, first_block):
    col_block = col_start // w_f
    assert col_block * w_f == col_start
    cc, sc = _dft_tables(GROUP_DIM)
    wc_hi, wc_lo = _np_split_bf16(np.concatenate([cc, sc], axis=1))
    cn, sn = _dft_tables(n_pos)
    cn_hi, cn_lo = _np_split_bf16(cn)
    sn_hi, sn_lo = _np_split_bf16(sn)
    const = lambda a: pl.BlockSpec(a.shape, lambda b: (0, 0))
    kern = functools.partial(_fourier_kernel, n_groups=w_f // GROUP_DIM,
                             scale=1.0 / math.sqrt(n_pos * GROUP_DIM))
    tables = (wc_hi, wc_lo, cn_hi, cn_lo, sn_hi, sn_lo)
    return pl.pallas_call(
        kern,
        name="fourier",
        grid=(n_batch,),
        in_specs=[pl.BlockSpec((n_pos, w_f), lambda b: (first_block + b, col_block))]
                 + [const(t) for t in tables],
        out_specs=pl.BlockSpec((n_pos, w_f), lambda b: (b, 0)),
        out_shape=jax.ShapeDtypeStruct((n_batch * n_pos, w_f), BF16),
        compiler_params=_params(("parallel",), 48),
    )(proj, *tables)


def _gelu(x):
    return 0.5 * x * (1.0 + lax.erf(x * math.sqrt(0.5)))


def _sgu_kernel(u_ref, v_ref, g_ref, w_ref, b_ref, o_ref, *, n_groups, n_chunks):
    for g in range(n_groups):
        cols = slice(g * GROUP_DIM, (g + 1) * GROUP_DIM)
        v = _rms(_gelu(v_ref[:, cols]), g_ref[:, cols]).astype(BF16)
        v_wide = jnp.concatenate([v[c * CHUNK:(c + 1) * CHUNK, :] for c in range(n_chunks)], axis=1)
        s = _dot(w_ref[g], v_wide) + b_ref[:, g:g + 1]
        for c in range(n_chunks):
            rws = slice(c * CHUNK, (c + 1) * CHUNK)
            o_ref[rws, cols] = (_gelu(u_ref[rws, cols])
                                * s[:, c * GROUP_DIM:(c + 1) * GROUP_DIM]).astype(o_ref.dtype)


def _spatial_gating(proj, col_start, w_c, g_sgu, w_sp, b_sp_t, layer):
    m = proj.shape[0]
    n_groups = w_c // GROUP_DIM
    n_chunks = 4
    tm = n_chunks * CHUNK
    col_block = col_start // w_c
    assert col_block * w_c == col_start
    kern = functools.partial(_sgu_kernel, n_groups=n_groups, n_chunks=n_chunks)
    return pl.pallas_call(
        kern,
        name="spatial_gate",
        grid=(m // tm,),
        in_specs=[
            pl.BlockSpec((tm, w_c), lambda i: (i, col_block)),
            pl.BlockSpec((tm, w_c), lambda i: (i, col_block + 1)),
            pl.BlockSpec((None, 1, w_c), lambda i: (layer, 0, 0)),
            pl.BlockSpec((None, n_groups, CHUNK, CHUNK), lambda i: (layer, 0, 0, 0)),
            pl.BlockSpec((None, CHUNK, n_groups), lambda i: (layer, 0, 0)),
        ],
        out_specs=pl.BlockSpec((tm, w_c), lambda i: (i, 0)),
        out_shape=jax.ShapeDtypeStruct((m, w_c), BF16),
        compiler_params=_params(("parallel",), 32),
    )(proj, proj, g_sgu, w_sp, b_sp_t)


def _merge_kernel(h_ref, oa_ctx_ref, oa_lat_ref, of_ctx_ref, of_lat_ref, oc_ref,
                  wga_ref, wgf_ref, wgc_ref, wa_ref, wf_ref, wc_ref, w1_ref, w2_ref,
                  o_ref, w1o_ref, w2o_ref, *, rows, tm):
    w1o_ref[...] = w1_ref[...].astype(BF16)
    w2o_ref[...] = w2_ref[...].astype(BF16)
    h = h_ref[...]
    o_a = rows.pick(tm, oa_ctx_ref, oa_lat_ref)
    o_f = rows.pick(tm, of_ctx_ref, of_lat_ref)
    acc = jax.nn.sigmoid(_dot(h, wga_ref[...])) * _dot(o_a, wa_ref[...])
    acc += jax.nn.sigmoid(_dot(h, wgf_ref[...])) * _dot(o_f, wf_ref[...])
    acc += jax.nn.sigmoid(_dot(h, wgc_ref[...])) * _dot(oc_ref[...], wc_ref[...])
    o_ref[...] = acc.astype(o_ref.dtype)


def _merge(rows, h, o_a_ctx, o_a_lat, o_f_ctx, o_f_lat, o_c, w_gates, w_br_a, w_br_f, w_br_c,
           w_mlp1, w_mlp2, layer):
    d = rows.d
    d_ff = w_mlp1.shape[2]
    tm, tn = 1024, 256
    nb = d // tn
    n_steps = (rows.m // tm) * nb
    n_slabs = 1 << (n_steps.bit_length() - 1)
    slab1, slab2 = d // n_slabs, d_ff // n_slabs
    assert slab1 * n_slabs == d and slab2 * n_slabs == d_ff and slab1 % 16 == 0

    def slab(i, n):
        return jnp.minimum(i * nb + n, n_slabs - 1)

    full = lambda width: pl.BlockSpec((tm, width), lambda i, n: (i, 0))
    gate_w = lambda j: pl.BlockSpec((None, d, tn), lambda i, n: (layer, 0, j * nb + n))
    br_w = lambda w: pl.BlockSpec((None, w.shape[1], tn), lambda i, n: (layer, 0, n))
    return pl.pallas_call(
        functools.partial(_merge_kernel, rows=rows, tm=tm),
        name="gate_merge",
        grid=(rows.m // tm, nb),
        in_specs=[full(d),
                  *rows.split_specs(tm, o_a_ctx.shape[1]),
                  *rows.split_specs(tm, o_f_ctx.shape[1]),
                  full(o_c.shape[1]),
                  gate_w(0), gate_w(1), gate_w(2),
                  br_w(w_br_a), br_w(w_br_f), br_w(w_br_c),
                  pl.BlockSpec((None, slab1, d_ff), lambda i, n: (layer, slab(i, n), 0)),
                  pl.BlockSpec((None, slab2, d), lambda i, n: (layer, slab(i, n), 0))],
        out_specs=[pl.BlockSpec((tm, tn), lambda i, n: (i, n)),
                   pl.BlockSpec((slab1, d_ff), lambda i, n: (slab(i, n), 0)),
                   pl.BlockSpec((slab2, d), lambda i, n: (slab(i, n), 0))],
        out_shape=[jax.ShapeDtypeStruct((rows.m, d), BF16),
                   jax.ShapeDtypeStruct((d, d_ff), BF16),
                   jax.ShapeDtypeStruct((d_ff, d), BF16)],
        compiler_params=_params(("arbitrary", "arbitrary"), 56),
    )(h, o_a_ctx, o_a_lat, o_f_ctx, o_f_lat, o_c, w_gates, w_gates, w_gates,
      w_br_a, w_br_f, w_br_c, w_mlp1, w_mlp2)


SUB_ROWS = 256


def _unit_rms(y):
    return y * lax.rsqrt(jnp.mean(y * y, axis=-1, keepdims=True) + RMS_EPS)


def _outproj_kernel(mg_ref, w_ref, x_ref, gpost_ref, gt_ref, gpre_ref, sc_ref, sh_ref,
                    xo_ref, ho_ref, y_even_ref, y_odd_ref, *, n_tiles):
    s = pl.program_id(0)
    y_refs = (y_even_ref, y_odd_ref)

    def multiply(parity):
        y_refs[parity][...] = _dot(mg_ref[...], w_ref[...])

    def finish(parity):
        post = gpost_ref[...] * gt_ref[...]
        pre = gpre_ref[...] * (1.0 + sc_ref[...])
        x1 = x_ref[...] + _unit_rms(y_refs[parity][...]) * post
        xo_ref[...] = x1
        ho_ref[...] = (_unit_rms(x1) * pre + sh_ref[...]).astype(ho_ref.dtype)

    @pl.when(s == 0)
    def _():
        multiply(0)

    for parity in (0, 1):
        @pl.when((s > 0) & (s < n_tiles) & (s % 2 == parity))
        def _():
            multiply(parity)
            finish(1 - parity)

    @pl.when(s == n_tiles)
    def _():
        finish((n_tiles - 1) % 2)


def _out_projection(rows, merged, w_out, x, g_post_mix, g_pre_mlp, mods, layer):
    tm = 512
    d = rows.d
    n_tiles = rows.m // tm
    ahead_spec = pl.BlockSpec((tm, d), lambda s: (jnp.minimum(s, n_tiles - 1), 0))
    lag_spec = pl.BlockSpec((tm, d), lambda s: (jnp.maximum(s - 1, 0), 0))
    return pl.pallas_call(
        functools.partial(_outproj_kernel, n_tiles=n_tiles),
        name="out_proj",
        grid=(n_tiles + 1,),
        in_specs=[
            ahead_spec,
            pl.BlockSpec((None, d, d), lambda s: (layer, 0, 0), pipeline_mode=pl.Buffered(1)),
            lag_spec,
            rows.gain_spec(layer),
            rows.mod_spec(layer, MOD_GATE1, tm, lag=1),
            rows.gain_spec(layer),
            rows.mod_spec(layer, MOD_SCALE2, tm, lag=1),
            rows.mod_spec(layer, MOD_SHIFT2, tm, lag=1),
        ],
        out_specs=[lag_spec, lag_spec],
        out_shape=[jax.ShapeDtypeStruct((rows.m, d), F32), jax.ShapeDtypeStruct((rows.m, d), BF16)],
        scratch_shapes=[pltpu.VMEM((tm, d), F32), pltpu.VMEM((tm, d), F32)],
        compiler_params=_params(("arbitrary",), 52),
    )(merged, w_out, x, g_post_mix, mods, g_pre_mlp, mods, mods)


def _mlp_kernel(*refs, n_ff_steps, has_next, rows, tm):
    if has_next:
        (h_ref, w1_ref, w2_ref, x_ref, gpost_ref, gt_ref, gpre_ref, sc_ref, sh_ref,
         xo_ref, ho_ref, acc_ref) = refs
    else:
        h_ref, w1_ref, w2_ref, x_ref, gpost_ref, gt_ref, yp_ref, ys_ref, acc_ref = refs
    step = pl.program_id(1)
    last = n_ff_steps - 1
    assert last >= 2

    def ff_slice(rws):
        hid = jnp.square(jnp.maximum(_dot(h_ref[rws, :], w1_ref[...]), 0.0)).astype(BF16)
        return _dot(hid, w2_ref[...])

    whole = slice(None)

    @pl.when(step == 0)
    def _():
        acc_ref[...] = ff_slice(whole)

    @pl.when((step > 0) & (step < last))
    def _():
        acc_ref[...] += ff_slice(whole)

    @pl.when(step == last)
    def _():
        post = gpost_ref[...] * gt_ref[...]
        if has_next:
            pre = gpre_ref[...] * (1.0 + sc_ref[...])
            shift = sh_ref[...]
        for c in range(tm // SUB_ROWS):
            rws = slice(c * SUB_ROWS, (c + 1) * SUB_ROWS)
            x2 = x_ref[rws, :] + _unit_rms(acc_ref[rws, :] + ff_slice(rws)) * post
            if has_next:
                xo_ref[rws, :] = x2
                ho_ref[rws, :] = (_unit_rms(x2) * pre + shift).astype(ho_ref.dtype)
            else:
                acc_ref[rws, :] = x2
        if not has_next:
            is_ctx = pl.program_id(0) < rows.m_ctx // tm

            @pl.when(is_ctx)
            def _():
                yp_ref[...] = acc_ref[...]

            @pl.when(jnp.logical_not(is_ctx))
            def _():
                ys_ref[...] = acc_ref[...]


def _mlp(rows, h2, w1, w2, x, g_post_mlp, g_pre_next, mods, layer, has_next):
    tm, tf = 512, 1024
    d = rows.d
    d_ff = w1.shape[1]
    n_ff_steps = d_ff // tf
    row_spec = pl.BlockSpec((tm, d), lambda i, f: (i, 0))
    in_specs = [
        row_spec,
        pl.BlockSpec((d, tf), lambda i, f: (0, f)),
        pl.BlockSpec((tf, d), lambda i, f: (f, 0)),
        row_spec,
        rows.gain_spec(layer),
        rows.mod_spec(layer, MOD_GATE2, tm),
    ]
    args = [h2, w1, w2, x, g_post_mlp, mods]
    if has_next:
        in_specs += [rows.gain_spec(layer + 1),
                     rows.mod_spec(layer + 1, MOD_SCALE1, tm),
                     rows.mod_spec(layer + 1, MOD_SHIFT1, tm)]
        args += [g_pre_next, mods, mods]
        out_specs = [row_spec, row_spec]
        out_shape = [jax.ShapeDtypeStruct((rows.m, d), F32),
                     jax.ShapeDtypeStruct((rows.m, d), BF16)]
    else:
        out_specs = list(rows.split_specs(tm, d))
        out_shape = [jax.ShapeDtypeStruct((rows.m_ctx, d), F32),
                     jax.ShapeDtypeStruct((rows.m_lat, d), F32)]
    return pl.pallas_call(
        functools.partial(_mlp_kernel, n_ff_steps=n_ff_steps, has_next=has_next, rows=rows, tm=tm),
        name="mlp",
        grid=(rows.m // tm, n_ff_steps),
        in_specs=in_specs,
        out_specs=out_specs,
        out_shape=out_shape,
        scratch_shapes=[pltpu.VMEM((tm, d), F32)],
        compiler_params=_params(("arbitrary", "arbitrary"), 56),
    )(*args)


def kernel(x_prompt, x_sample, cache_k, cache_v, c, c_ctx, w_ada, b_ada, g_pre_mix, g_post_mix,
           g_pre_mlp, g_post_mlp, w_in, rpb, g_sgu, w_spatial, b_spatial, w_br_a, w_br_f, w_br_c,
           w_out, w_mlp1, w_mlp2):
    batch, seq, d = x_prompt.shape
    dec_batch, dec_seq, _ = x_sample.shape
    depth = w_in.shape[0]
    past = cache_k.shape[2]
    w_a = w_br_a.shape[1]
    w_f = w_br_f.shape[1]
    w_c = w_br_c.shape[1]
    n_heads = w_a // HEAD_DIM
    m_ctx = batch * seq
    rows = _Rows(m_ctx, dec_batch, dec_seq, d)
    assert dec_seq % GRID_W == 0 and m_ctx % dec_seq == 0 and MOD_ROWS >= 1 + dec_batch

    cvec = jnp.zeros((MOD_ROWS, d), F32).at[0].set(c_ctx).at[1:1 + dec_batch].set(c)
    mods = _ada(cvec, w_ada, b_ada)
    mods = mods[:, :1 + dec_batch].reshape(depth, 1 + dec_batch, N_MOD, d)
    mods = mods.transpose(0, 2, 1, 3)[:, :, :, None, :]

    col_f = 3 * w_a
    col_uv = col_f + w_f
    col_gates = col_uv + 2 * w_c

    w_gates_b = w_in[:, :, col_gates:].astype(BF16)
    w_br_a_b, w_br_f_b, w_br_c_b = w_br_a.astype(BF16), w_br_f.astype(BF16), w_br_c.astype(BF16)
    w_out_b = w_out.astype(BF16)
    w_sp_b = w_spatial.astype(BF16)
    b_sp_t = b_spatial.transpose(0, 2, 1)
    g_sgu_row = g_sgu.reshape(depth, 1, w_c)
    gains = [g.reshape(depth, 1, d) for g in (g_pre_mix, g_post_mix, g_pre_mlp, g_post_mlp)]
    g_pre_mix_r, g_post_mix_r, g_pre_mlp_r, g_post_mlp_r = gains
    cache_k_r = cache_k.reshape(dec_batch, depth, past, w_a)
    cache_v_r = cache_v.reshape(dec_batch, depth, past, w_a)
    bias = _window_bias(rpb, dec_seq // GRID_W)

    x, h = _prologue(rows, x_prompt.reshape(m_ctx, d), x_sample.reshape(rows.m_lat, d),
                     g_pre_mix_r, mods, 0)
    new_k = jnp.zeros((batch, depth, seq, w_a), F32)
    new_v = jnp.zeros((batch, depth, seq, w_a), F32)
    for l in range(depth):
        proj = _project(h, w_in, l, col_gates, 1536, F32)
        o_a_ctx, new_k, new_v = _ctx_attention(proj, new_k, new_v, l, depth, batch, seq, w_a)
        o_a_lat = _lat_attention(proj, cache_k_r, cache_v_r, bias, l, m_ctx, dec_batch, dec_seq, w_a)
        o_f_ctx = _fourier(proj, col_f, w_f, batch, seq, 0)
        o_f_lat = _fourier(proj, col_f, w_f, dec_batch, dec_seq, m_ctx // dec_seq)
        o_c = _spatial_gating(proj, col_uv, w_c, g_sgu_row, w_sp_b, b_sp_t, l)
        merged, w1_b, w2_b = _merge(rows, h, o_a_ctx, o_a_lat, o_f_ctx, o_f_lat, o_c, w_gates_b,
                                    w_br_a_b, w_br_f_b, w_br_c_b, w_mlp1, w_mlp2, l)
        x, h2 = _out_projection(rows, merged, w_out_b, x, g_post_mix_r, g_pre_mlp_r, mods, l)
        x, h = _mlp(rows, h2, w1_b, w2_b, x, g_post_mlp_r, g_pre_mix_r, mods, l, l + 1 < depth)

    y_p = x.reshape(batch, seq, d)
    y_s = h.reshape(dec_batch, dec_seq, d)
    cache_shape = (batch, depth, seq, n_heads, HEAD_DIM)
    return (y_p, y_s, new_k.reshape(cache_shape), new_v.reshape(cache_shape))
```

```python
import functools
import math

import numpy as np
import jax
import jax.numpy as jnp
from jax import lax
from jax.experimental import pallas as pl
from jax.experimental.pallas import tpu as pltpu

F32 = jnp.float32
BF16 = jnp.bfloat16

RMS_EPS = 1e-6
MASKED_SCORE = -1e30

HEAD_DIM = 128
GROUP_DIM = 128
GRID_W = 64
CHUNK = 128
MAX_WIN_R = 8
WIN_C = 16
N_MOD = 6
MOD_SHIFT1, MOD_SCALE1, MOD_GATE1, MOD_SHIFT2, MOD_SCALE2, MOD_GATE2 = range(N_MOD)
LANES, SUBLANES = 128, 8
MOD_ROWS = SUBLANES

MIB = 2 ** 20


def _params(semantics, vmem_mib):
    return pltpu.CompilerParams(dimension_semantics=semantics, vmem_limit_bytes=vmem_mib * MIB)


def _rms(x, g):
    return x * lax.rsqrt(jnp.mean(x * x, axis=-1, keepdims=True) + RMS_EPS) * g


def _dot(a, b):
    return jnp.dot(a, b, preferred_element_type=F32)


def _dot_nt(a, b):
    return lax.dot_general(a, b, (((1,), (1,)), ((), ())), preferred_element_type=F32)


def _ada_kernel(c_ref, w_ref, b_ref, o_ref):
    c = c_ref[...]
    s = c * jax.nn.sigmoid(c)
    o_ref[...] = _dot(s.astype(BF16), w_ref[...].astype(BF16)) + b_ref[...]


def _ada(cvec, w_ada, b_ada):
    depth, d, width = w_ada.shape
    tn = 1024
    return pl.pallas_call(
        _ada_kernel,
        name="ada",
        grid=(depth, width // tn),
        in_specs=[
            pl.BlockSpec((MOD_ROWS, d), lambda l, n: (0, 0)),
            pl.BlockSpec((None, d, tn), lambda l, n: (l, 0, n)),
            pl.BlockSpec((None, 1, tn), lambda l, n: (l, 0, n)),
        ],
        out_specs=pl.BlockSpec((None, MOD_ROWS, tn), lambda l, n: (l, 0, n)),
        out_shape=jax.ShapeDtypeStruct((depth, MOD_ROWS, width), F32),
        compiler_params=_params(("parallel", "parallel"), 40),
    )(cvec, w_ada, b_ada.reshape(depth, 1, width))


class _Rows:
    def __init__(self, m_ctx, dec_batch, dec_seq, d):
        self.m_ctx, self.dec_batch, self.dec_seq, self.d = m_ctx, dec_batch, dec_seq, d
        self.m_lat = dec_batch * dec_seq
        self.m = m_ctx + self.m_lat

    def group(self, i, tm):
        n_ctx = self.m_ctx // tm
        per_batch = self.dec_seq // tm
        return jnp.where(i < n_ctx, 0, 1 + (i - n_ctx) // per_batch)

    def mod_spec(self, layer, piece, tm, lag=0):
        return pl.BlockSpec((None, None, None, 1, self.d),
                            lambda i, *_: (layer, piece, self.group(jnp.maximum(i - lag, 0), tm), 0, 0))

    def gain_spec(self, layer):
        return pl.BlockSpec((None, 1, self.d), lambda i, *_: (layer, 0, 0))

    def split_specs(self, tm, width):
        n_ctx = self.m_ctx // tm
        ctx = pl.BlockSpec((tm, width), lambda i, *_: (jnp.minimum(i, n_ctx - 1), 0))
        lat = pl.BlockSpec((tm, width), lambda i, *_: (jnp.maximum(i - n_ctx, 0), 0))
        return ctx, lat

    def pick(self, tm, ctx_ref, lat_ref):
        return jnp.where(pl.program_id(0) < self.m_ctx // tm, ctx_ref[...], lat_ref[...])


def _prologue_kernel(xp_ref, xs_ref, g_ref, sc_ref, sh_ref, x_ref, h_ref, *, rows, tm):
    x = rows.pick(tm, xp_ref, xs_ref)
    x_ref[...] = x
    h = _rms(x, g_ref[...]) * (1.0 + sc_ref[...]) + sh_ref[...]
    h_ref[...] = h.astype(h_ref.dtype)


def _prologue(rows, x_ctx, x_lat, g_pre, mods, layer):
    tm = 512
    row_spec = pl.BlockSpec((tm, rows.d), lambda i: (i, 0))
    return pl.pallas_call(
        functools.partial(_prologue_kernel, rows=rows, tm=tm),
        name="prologue",
        grid=(rows.m // tm,),
        in_specs=[
            *rows.split_specs(tm, rows.d),
            rows.gain_spec(layer),
            rows.mod_spec(layer, MOD_SCALE1, tm),
            rows.mod_spec(layer, MOD_SHIFT1, tm),
        ],
        out_specs=[row_spec, row_spec],
        out_shape=[jax.ShapeDtypeStruct((rows.m, rows.d), F32),
                   jax.ShapeDtypeStruct((rows.m, rows.d), BF16)],
        compiler_params=_params(("arbitrary",), 40),
    )(x_ctx, x_lat, g_pre, mods, mods)


def _inproj_kernel(a_ref, w_ref, o_ref, wb_ref):
    @pl.when(pl.program_id(1) == 0)
    def _():
        wb_ref[...] = w_ref[...].astype(BF16)

    o_ref[...] = _dot(a_ref[...], wb_ref[...]).astype(o_ref.dtype)


def _project(h, w, layer, width, tn, out_dtype):
    m, k = h.shape
    tm = 1024
    assert width % tn == 0
    return pl.pallas_call(
        _inproj_kernel,
        name="in_proj",
        grid=(width // tn, m // tm),
        in_specs=[
            pl.BlockSpec((tm, k), lambda n, i: (i, 0)),
            pl.BlockSpec((None, k, tn), lambda n, i: (layer, 0, n)),
        ],
        out_specs=pl.BlockSpec((tm, tn), lambda n, i: (i, n)),
        out_shape=jax.ShapeDtypeStruct((m, width), out_dtype),
        scratch_shapes=[pltpu.VMEM((k, tn), BF16)],
        compiler_params=_params(("arbitrary", "arbitrary"), 58),
    )(h, w)


def _ctx_attn_kernel(q_ref, k_ref, v_ref, o_ref, ko_ref, vo_ref, *, n_heads, scale):
    ko_ref[...] = k_ref[...]
    vo_ref[...] = v_ref[...]
    for h in range(n_heads):
        cols = slice(h * HEAD_DIM, (h + 1) * HEAD_DIM)
        q = q_ref[:, cols].astype(BF16)
        k = k_ref[:, cols].astype(BF16)
        v = v_ref[:, cols].astype(BF16)
        s = _dot_nt(q, k) * scale
        p = jnp.exp(s - jnp.max(s, axis=-1, keepdims=True))
        denom = jnp.sum(p, axis=-1, keepdims=True)
        o_ref[:, cols] = (_dot(p.astype(BF16), v) / denom).astype(o_ref.dtype)


def _ctx_attention(proj, new_k, new_v, layer, depth, batch, seq, w_a):
    n_heads = w_a // HEAD_DIM
    kern = functools.partial(_ctx_attn_kernel, n_heads=n_heads, scale=1.0 / math.sqrt(HEAD_DIM))
    cache_spec = pl.BlockSpec((None, None, seq, w_a), lambda b: (b, layer, 0, 0))
    cache_shape = jax.ShapeDtypeStruct((batch, depth, seq, w_a), F32)
    in_specs = [pl.BlockSpec((seq, w_a), lambda b, j=j: (b, j)) for j in range(3)]
    in_specs += [pl.BlockSpec(memory_space=pl.ANY)] * 2

    def body(q_ref, k_ref, v_ref, k_all_ref, v_all_ref, o_ref, ko_ref, vo_ref):
        del k_all_ref, v_all_ref
        kern(q_ref, k_ref, v_ref, o_ref, ko_ref, vo_ref)

    return pl.pallas_call(
        body,
        name="ctx_attn",
        grid=(batch,),
        in_specs=in_specs,
        out_specs=[pl.BlockSpec((seq, w_a), lambda b: (b, 0)), cache_spec, cache_spec],
        out_shape=[jax.ShapeDtypeStruct((batch * seq, w_a), BF16), cache_shape, cache_shape],
        input_output_aliases={3: 1, 4: 2},
        compiler_params=_params(("arbitrary",), 32),
    )(proj, proj, proj, new_k, new_v)


def _window_start(r, rows):
    win_r = min(MAX_WIN_R, rows)
    return min(max(r - win_r // 2, 0), rows - win_r)


QUERY_GROUP_ROWS = 4


def _query_groups(rows):
    win_r = min(MAX_WIN_R, rows)
    span = min(rows, win_r + QUERY_GROUP_ROWS)
    assert rows % QUERY_GROUP_ROWS == 0
    groups = []
    for r0 in range(0, rows, QUERY_GROUP_ROWS):
        k0 = min(_window_start(r0, rows), rows - span)
        for r in range(r0, r0 + QUERY_GROUP_ROWS):
            assert k0 <= _window_start(r, rows) and _window_start(r, rows) + win_r <= k0 + span
        groups.append((r0, k0))
    return groups, span


def _bias_kernel(rpb_ref, o_ref, *, rows):
    win_r = min(MAX_WIN_R, rows)
    lanes = rpb_ref.shape[1]
    q = lax.broadcasted_iota(jnp.int32, (GRID_W, lanes), 0)
    kc = lax.broadcasted_iota(jnp.int32, (GRID_W, lanes), 1)
    c_start = jnp.clip(q - WIN_C // 2, 0, GRID_W - WIN_C)
    valid = (kc >= c_start) & (kc < c_start + WIN_C)
    tables = []
    for dr in range(2 * MAX_WIN_R - 1):
        row = jnp.broadcast_to(rpb_ref[dr:dr + 1, :], (GRID_W, lanes))
        t = pltpu.roll(row, lanes - (WIN_C - 1), 1, stride=1, stride_axis=0)
        tables.append(jnp.where(valid, t, MASKED_SCORE)[:, :GRID_W])
    masked = jnp.full((GRID_W, GRID_W), MASKED_SCORE, F32)
    groups, span = _query_groups(rows)
    for g, (r0, k0) in enumerate(groups):
        for j in range(QUERY_GROUP_ROWS):
            start = _window_start(r0 + j, rows)
            for i in range(span):
                in_window = start <= k0 + i < start + win_r
                dr = k0 + i - (r0 + j) + (MAX_WIN_R - 1)
                o_ref[g, j * GRID_W:(j + 1) * GRID_W, i * GRID_W:(i + 1) * GRID_W] = (
                    tables[dr] if in_window else masked)


def _window_bias(rpb, rows):
    depth, n_heads, n_dr, n_dc = rpb.shape
    groups, span = _query_groups(rows)
    block = (len(groups), QUERY_GROUP_ROWS * GRID_W, span * GRID_W)
    dr_pad = -(-n_dr // SUBLANES) * SUBLANES
    rpb_rows = jnp.pad(rpb.reshape(depth * n_heads, n_dr, n_dc),
                       ((0, 0), (0, dr_pad - n_dr), (0, LANES - n_dc)))
    return pl.pallas_call(
        functools.partial(_bias_kernel, rows=rows),
        name="window_bias",
        grid=(depth * n_heads,),
        in_specs=[pl.BlockSpec((None, dr_pad, LANES), lambda i: (i, 0, 0))],
        out_specs=pl.BlockSpec((None,) + block, lambda i: (i, 0, 0, 0)),
        out_shape=jax.ShapeDtypeStruct((depth * n_heads,) + block, F32),
        compiler_params=_params(("parallel",), 32),
    )(rpb_rows)


def _lat_attn_kernel(q_ref, k_ref, v_ref, ck_ref, cv_ref, bias_ref, o_ref, *, rows, scale):
    groups, span = _query_groups(rows)
    ck = ck_ref[...].astype(BF16)
    cv = cv_ref[...].astype(BF16)
    for g, (r0, k0) in enumerate(groups):
        q_rows = slice(r0 * GRID_W, (r0 + QUERY_GROUP_ROWS) * GRID_W)
        k_rows = slice(k0 * GRID_W, (k0 + span) * GRID_W)
        q = q_ref[q_rows, :].astype(BF16)
        kw = k_ref[k_rows, :].astype(BF16)
        vw = v_ref[k_rows, :].astype(BF16)
        s_lat = _dot_nt(q, kw) * scale + bias_ref[g]
        s_ctx = _dot_nt(q, ck) * scale
        top = jnp.maximum(jnp.max(s_lat, axis=-1, keepdims=True),
                          jnp.max(s_ctx, axis=-1, keepdims=True))
        p_lat = jnp.exp(s_lat - top)
        p_ctx = jnp.exp(s_ctx - top)
        denom = jnp.sum(p_lat, axis=-1, keepdims=True) + jnp.sum(p_ctx, axis=-1, keepdims=True)
        o = _dot(p_lat.astype(BF16), vw) + _dot(p_ctx.astype(BF16), cv)
        o_ref[q_rows, :] = (o / denom).astype(o_ref.dtype)


def _lat_attention(proj, cache_k, cache_v, bias, layer, m_ctx, dec_batch, dec_seq, w_a):
    n_heads = w_a // HEAD_DIM
    rows = dec_seq // GRID_W
    past = cache_k.shape[2]
    first = m_ctx // dec_seq
    assert first * dec_seq == m_ctx
    kern = functools.partial(_lat_attn_kernel, rows=rows, scale=1.0 / math.sqrt(HEAD_DIM))
    qkv_specs = [pl.BlockSpec((dec_seq, HEAD_DIM), lambda h, b, j=j: (first + b, j * n_heads + h))
                 for j in range(3)]
    cache_spec = pl.BlockSpec((None, None, past, HEAD_DIM), lambda h, b: (b, layer, 0, h))
    return pl.pallas_call(
        kern,
        name="lat_attn",
        grid=(n_heads, dec_batch),
        in_specs=qkv_specs + [
            cache_spec, cache_spec,
            pl.BlockSpec((None,) + bias.shape[1:], lambda h, b: (layer * n_heads + h, 0, 0, 0)),
        ],
        out_specs=pl.BlockSpec((dec_seq, HEAD_DIM), lambda h, b: (b, h)),
        out_shape=jax.ShapeDtypeStruct((dec_batch * dec_seq, w_a), BF16),
        compiler_params=_params(("parallel", "parallel"), 32),
    )(proj, proj, proj, cache_k, cache_v, bias)


def _dft_tables(n):
    idx = np.arange(n, dtype=np.int64)
    ang = 2.0 * np.pi * ((idx[:, None] * idx[None, :]) % n) / n
    return np.cos(ang), np.sin(ang)


def _np_split_bf16(a):
    a32 = np.asarray(a, np.float32)
    hi = a32.astype(BF16)
    lo = (a32 - hi.astype(np.float32)).astype(BF16)
    return jnp.asarray(hi), jnp.asarray(lo)


def _fourier_kernel(f_ref, wc_hi_ref, wc_lo_ref, cn_hi_ref, cn_lo_ref, sn_hi_ref, sn_lo_ref,
                    o_ref, *, n_groups, scale):
    wc_hi, wc_lo = wc_hi_ref[...], wc_lo_ref[...]
    pc, ps = [], []
    for g in range(n_groups):
        x = f_ref[:, g * GROUP_DIM:(g + 1) * GROUP_DIM].astype(BF16)
        p = _dot(x, wc_hi) + _dot(x, wc_lo)
        pc.append(p[:, :GROUP_DIM])
        ps.append(p[:, GROUP_DIM:])

    def left2(m_hi_ref, m_lo_ref, b):
        b = b.astype(BF16)
        return _dot(m_hi_ref[...], b) + _dot(m_lo_ref[...], b)

    y = (left2(cn_hi_ref, cn_lo_ref, jnp.concatenate(pc, axis=1))
         - left2(sn_hi_ref, sn_lo_ref, jnp.concatenate(ps, axis=1)))
    o_ref[...] = (y * scale).astype(o_ref.dtype)


def _fourier(proj, col_start, w_f, n_batch, n_pos, first_block):
    col_block = col_start // w_f
    assert col_block * w_f == col_start
    cc, sc = _dft_tables(GROUP_DIM)
    wc_hi, wc_lo = _np_split_bf16(np.concatenate([cc, sc], axis=1))
    cn, sn = _dft_tables(n_pos)
    cn_hi, cn_lo = _np_split_bf16(cn)
    sn_hi, sn_lo = _np_split_bf16(sn)
    const = lambda a: pl.BlockSpec(a.shape, lambda b: (0, 0))
    kern = functools.partial(_fourier_kernel, n_groups=w_f // GROUP_DIM,
                             scale=1.0 / math.sqrt(n_pos * GROUP_DIM))
    tables = (wc_hi, wc_lo, cn_hi, cn_lo, sn_hi, sn_lo)
    return pl.pallas_call(
        kern,
        name="fourier",
        grid=(n_batch,),
        in_specs=[pl.BlockSpec((n_pos, w_f), lambda b: (first_block + b, col_block))]
                 + [const(t) for t in tables],
        out_specs=pl.BlockSpec((n_pos, w_f), lambda b: (b, 0)),
        out_shape=jax.ShapeDtypeStruct((n_batch * n_pos, w_f), BF16),
        compiler_params=_params(("parallel",), 48),
    )(proj, *tables)


def _gelu(x):
    return 0.5 * x * (1.0 + lax.erf(x * math.sqrt(0.5)))


def _sgu_kernel(u_ref, v_ref, g_ref, w_ref, b_ref, o_ref, *, n_groups, n_chunks):
    for g in range(n_groups):
        cols = slice(g * GROUP_DIM, (g + 1) * GROUP_DIM)
        v = _rms(_gelu(v_ref[:, cols]), g_ref[:, cols]).astype(BF16)
        v_wide = jnp.concatenate([v[c * CHUNK:(c + 1) * CHUNK, :] for c in range(n_chunks)], axis=1)
        s = _dot(w_ref[g], v_wide) + b_ref[:, g:g + 1]
        for c in range(n_chunks):
            rws = slice(c * CHUNK, (c + 1) * CHUNK)
            o_ref[rws, cols] = (_gelu(u_ref[rws, cols])
                                * s[:, c * GROUP_DIM:(c + 1) * GROUP_DIM]).astype(o_ref.dtype)


def _spatial_gating(proj, col_start, w_c, g_sgu, w_sp, b_sp_t, layer):
    m = proj.shape[0]
    n_groups = w_c // GROUP_DIM
    n_chunks = 4
    tm = n_chunks * CHUNK
    col_block = col_start // w_c
    assert col_block * w_c == col_start
    kern = functools.partial(_sgu_kernel, n_groups=n_groups, n_chunks=n_chunks)
    return pl.pallas_call(
        kern,
        name="spatial_gate",
        grid=(m // tm,),
        in_specs=[
            pl.BlockSpec((tm, w_c), lambda i: (i, col_block)),
            pl.BlockSpec((tm, w_c), lambda i: (i, col_block + 1)),
            pl.BlockSpec((None, 1, w_c), lambda i: (layer, 0, 0)),
            pl.BlockSpec((None, n_groups, CHUNK, CHUNK), lambda i: (layer, 0, 0, 0)),
            pl.BlockSpec((None, CHUNK, n_groups), lambda i: (layer, 0, 0)),
        ],
        out_specs=pl.BlockSpec((tm, w_c), lambda i: (i, 0)),
        out_shape=jax.ShapeDtypeStruct((m, w_c), BF16),
        compiler_params=_params(("parallel",), 32),
    )(proj, proj, g_sgu, w_sp, b_sp_t)


def _merge_kernel(h_ref, oa_ctx_ref, oa_lat_ref, of_ctx_ref, of_lat_ref, oc_ref,
                  wga_ref, wgf_ref, wgc_ref, wa_ref, wf_ref, wc_ref, w1_ref, w2_ref,
                  o_ref, w1o_ref, w2o_ref, *, rows, tm):
    w1o_ref[...] = w1_ref[...].astype(BF16)
    w2o_ref[...] = w2_ref[...].astype(BF16)
    h = h_ref[...]
    o_a = rows.pick(tm, oa_ctx_ref, oa_lat_ref)
    o_f = rows.pick(tm, of_ctx_ref, of_lat_ref)
    acc = jax.nn.sigmoid(_dot(h, wga_ref[...])) * _dot(o_a, wa_ref[...])
    acc += jax.nn.sigmoid(_dot(h, wgf_ref[...])) * _dot(o_f, wf_ref[...])
    acc += jax.nn.sigmoid(_dot(h, wgc_ref[...])) * _dot(oc_ref[...], wc_ref[...])
    o_ref[...] = acc.astype(o_ref.dtype)


def _merge(rows, h, o_a_ctx, o_a_lat, o_f_ctx, o_f_lat, o_c, w_gates, w_br_a, w_br_f, w_br_c,
           w_mlp1, w_mlp2, layer):
    d = rows.d
    d_ff = w_mlp1.shape[2]
    tm, tn = 1024, 256
    nb = d // tn
    n_steps = (rows.m // tm) * nb
    n_slabs = 1 << (n_steps.bit_length() - 1)
    slab1, slab2 = d // n_slabs, d_ff // n_slabs
    assert slab1 * n_slabs == d and slab2 * n_slabs == d_ff and slab1 % 16 == 0

    def slab(i, n):
        return jnp.minimum(i * nb + n, n_slabs - 1)

    full = lambda width: pl.BlockSpec((tm, width), lambda i, n: (i, 0))
    gate_w = lambda j: pl.BlockSpec((None, d, tn), lambda i, n: (layer, 0, j * nb + n))
    br_w = lambda w: pl.BlockSpec((None, w.shape[1], tn), lambda i, n: (layer, 0, n))
    return pl.pallas_call(
        functools.partial(_merge_kernel, rows=rows, tm=tm),
        name="gate_merge",
        grid=(rows.m // tm, nb),
        in_specs=[full(d),
                  *rows.split_specs(tm, o_a_ctx.shape[1]),
                  *rows.split_specs(tm, o_f_ctx.shape[1]),
                  full(o_c.shape[1]),
                  gate_w(0), gate_w(1), gate_w(2),
                  br_w(w_br_a), br_w(w_br_f), br_w(w_br_c),
                  pl.BlockSpec((None, slab1, d_ff), lambda i, n: (layer, slab(i, n), 0)),
                  pl.BlockSpec((None, slab2, d), lambda i, n: (layer, slab(i, n), 0))],
        out_specs=[pl.BlockSpec((tm, tn), lambda i, n: (i, n)),
                   pl.BlockSpec((slab1, d_ff), lambda i, n: (slab(i, n), 0)),
                   pl.BlockSpec((slab2, d), lambda i, n: (slab(i, n), 0))],
        out_shape=[jax.ShapeDtypeStruct((rows.m, d), BF16),
                   jax.ShapeDtypeStruct((d, d_ff), BF16),
                   jax.ShapeDtypeStruct((d_ff, d), BF16)],
        compiler_params=_params(("arbitrary", "arbitrary"), 56),
    )(h, o_a_ctx, o_a_lat, o_f_ctx, o_f_lat, o_c, w_gates, w_gates, w_gates,
      w_br_a, w_br_f, w_br_c, w_mlp1, w_mlp2)


SUB_ROWS = 256


def _unit_rms(y):
    return y * lax.rsqrt(jnp.mean(y * y, axis=-1, keepdims=True) + RMS_EPS)


def _outproj_kernel(mg_ref, w_ref, x_ref, gpost_ref, gt_ref, gpre_ref, sc_ref, sh_ref,
                    xo_ref, ho_ref, y_even_ref, y_odd_ref, *, n_tiles):
    s = pl.program_id(0)
    y_refs = (y_even_ref, y_odd_ref)

    def multiply(parity):
        y_refs[parity][...] = _dot(mg_ref[...], w_ref[...])

    def finish(parity):
        post = gpost_ref[...] * gt_ref[...]
        pre = gpre_ref[...] * (1.0 + sc_ref[...])
        x1 = x_ref[...] + _unit_rms(y_refs[parity][...]) * post
        xo_ref[...] = x1
        ho_ref[...] = (_unit_rms(x1) * pre + sh_ref[...]).astype(ho_ref.dtype)

    @pl.when(s == 0)
    def _():
        multiply(0)

    for parity in (0, 1):
        @pl.when((s > 0) & (s < n_tiles) & (s % 2 == parity))
        def _():
            multiply(parity)
            finish(1 - parity)

    @pl.when(s == n_tiles)
    def _():
        finish((n_tiles - 1) % 2)


def _out_projection(rows, merged, w_out, x, g_post_mix, g_pre_mlp, mods, layer):
    tm = 512
    d = rows.d
    n_tiles = rows.m // tm
    ahead_spec = pl.BlockSpec((tm, d), lambda s: (jnp.minimum(s, n_tiles - 1), 0))
    lag_spec = pl.BlockSpec((tm, d), lambda s: (jnp.maximum(s - 1, 0), 0))
    return pl.pallas_call(
        functools.partial(_outproj_kernel, n_tiles=n_tiles),
        name="out_proj",
        grid=(n_tiles + 1,),
        in_specs=[
            ahead_spec,
            pl.BlockSpec((None, d, d), lambda s: (layer, 0, 0), pipeline_mode=pl.Buffered(1)),
            lag_spec,
            rows.gain_spec(layer),
            rows.mod_spec(layer, MOD_GATE1, tm, lag=1),
            rows.gain_spec(layer),
            rows.mod_spec(layer, MOD_SCALE2, tm, lag=1),
            rows.mod_spec(layer, MOD_SHIFT2, tm, lag=1),
        ],
        out_specs=[lag_spec, lag_spec],
        out_shape=[jax.ShapeDtypeStruct((rows.m, d), F32), jax.ShapeDtypeStruct((rows.m, d), BF16)],
        scratch_shapes=[pltpu.VMEM((tm, d), F32), pltpu.VMEM((tm, d), F32)],
        compiler_params=_params(("arbitrary",), 52),
    )(merged, w_out, x, g_post_mix, mods, g_pre_mlp, mods, mods)


def _mlp_kernel(*refs, n_ff_steps, has_next, rows, tm):
    if has_next:
        (h_ref, w1_ref, w2_ref, x_ref, gpost_ref, gt_ref, gpre_ref, sc_ref, sh_ref,
         xo_ref, ho_ref, acc_ref) = refs
    else:
        h_ref, w1_ref, w2_ref, x_ref, gpost_ref, gt_ref, yp_ref, ys_ref, acc_ref = refs
    step = pl.program_id(1)
    last = n_ff_steps - 1
    assert last >= 2

    def ff_slice(rws):
        hid = jnp.square(jnp.maximum(_dot(h_ref[rws, :], w1_ref[...]), 0.0)).astype(BF16)
        return _dot(hid, w2_ref[...])

    whole = slice(None)

    @pl.when(step == 0)
    def _():
        acc_ref[...] = ff_slice(whole)

    @pl.when((step > 0) & (step < last))
    def _():
        acc_ref[...] += ff_slice(whole)

    @pl.when(step == last)
    def _():
        post = gpost_ref[...] * gt_ref[...]
        if has_next:
            pre = gpre_ref[...] * (1.0 + sc_ref[...])
            shift = sh_ref[...]
        for c in range(tm // SUB_ROWS):
            rws = slice(c * SUB_ROWS, (c + 1) * SUB_ROWS)
            x2 = x_ref[rws, :] + _unit_rms(acc_ref[rws, :] + ff_slice(rws)) * post
            if has_next:
                xo_ref[rws, :] = x2
                ho_ref[rws, :] = (_unit_rms(x2) * pre + shift).astype(ho_ref.dtype)
            else:
                acc_ref[rws, :] = x2
        if not has_next:
            is_ctx = pl.program_id(0) < rows.m_ctx // tm

            @pl.when(is_ctx)
            def _():
                yp_ref[...] = acc_ref[...]

            @pl.when(jnp.logical_not(is_ctx))
            def _():
                ys_ref[...] = acc_ref[...]


def _mlp(rows, h2, w1, w2, x, g_post_mlp, g_pre_next, mods, layer, has_next):
    tm, tf = 512, 1024
    d = rows.d
    d_ff = w1.shape[1]
    n_ff_steps = d_ff // tf
    row_spec = pl.BlockSpec((tm, d), lambda i, f: (i, 0))
    in_specs = [
        row_spec,
        pl.BlockSpec((d, tf), lambda i, f: (0, f)),
        pl.BlockSpec((tf, d), lambda i, f: (f, 0)),
        row_spec,
        rows.gain_spec(layer),
        rows.mod_spec(layer, MOD_GATE2, tm),
    ]
    args = [h2, w1, w2, x, g_post_mlp, mods]
    if has_next:
        in_specs += [rows.gain_spec(layer + 1),
                     rows.mod_spec(layer + 1, MOD_SCALE1, tm),
                     rows.mod_spec(layer + 1, MOD_SHIFT1, tm)]
        args += [g_pre_next, mods, mods]
        out_specs = [row_spec, row_spec]
        out_shape = [jax.ShapeDtypeStruct((rows.m, d), F32),
                     jax.ShapeDtypeStruct((rows.m, d), BF16)]
    else:
        out_specs = list(rows.split_specs(tm, d))
        out_shape = [jax.ShapeDtypeStruct((rows.m_ctx, d), F32),
                     jax.ShapeDtypeStruct((rows.m_lat, d), F32)]
    return pl.pallas_call(
        functools.partial(_mlp_kernel, n_ff_steps=n_ff_steps, has_next=has_next, rows=rows, tm=tm),
        name="mlp",
        grid=(rows.m // tm, n_ff_steps),
        in_specs=in_specs,
        out_specs=out_specs,
        out_shape=out_shape,
        scratch_shapes=[pltpu.VMEM((tm, d), F32)],
        compiler_params=_params(("arbitrary", "arbitrary"), 56),
    )(*args)


def kernel(x_prompt, x_sample, cache_k, cache_v, c, c_ctx, w_ada, b_ada, g_pre_mix, g_post_mix,
           g_pre_mlp, g_post_mlp, w_in, rpb, g_sgu, w_spatial, b_spatial, w_br_a, w_br_f, w_br_c,
           w_out, w_mlp1, w_mlp2):
    batch, seq, d = x_prompt.shape
    dec_batch, dec_seq, _ = x_sample.shape
    depth = w_in.shape[0]
    past = cache_k.shape[2]
    w_a = w_br_a.shape[1]
    w_f = w_br_f.shape[1]
    w_c = w_br_c.shape[1]
    n_heads = w_a // HEAD_DIM
    m_ctx = batch * seq
    rows = _Rows(m_ctx, dec_batch, dec_seq, d)
    assert dec_seq % GRID_W == 0 and m_ctx % dec_seq == 0 and MOD_ROWS >= 1 + dec_batch

    cvec = jnp.zeros((MOD_ROWS, d), F32).at[0].set(c_ctx).at[1:1 + dec_batch].set(c)
    mods = _ada(cvec, w_ada, b_ada)
    mods = mods[:, :1 + dec_batch].reshape(depth, 1 + dec_batch, N_MOD, d)
    mods = mods.transpose(0, 2, 1, 3)[:, :, :, None, :]

    col_f = 3 * w_a
    col_uv = col_f + w_f
    col_gates = col_uv + 2 * w_c

    w_gates_b = w_in[:, :, col_gates:].astype(BF16)
    w_br_a_b, w_br_f_b, w_br_c_b = w_br_a.astype(BF16), w_br_f.astype(BF16), w_br_c.astype(BF16)
    w_out_b = w_out.astype(BF16)
    w_sp_b = w_spatial.astype(BF16)
    b_sp_t = b_spatial.transpose(0, 2, 1)
    g_sgu_row = g_sgu.reshape(depth, 1, w_c)
    gains = [g.reshape(depth, 1, d) for g in (g_pre_mix, g_post_mix, g_pre_mlp, g_post_mlp)]
    g_pre_mix_r, g_post_mix_r, g_pre_mlp_r, g_post_mlp_r = gains
    cache_k_r = cache_k.reshape(dec_batch, depth, past, w_a)
    cache_v_r = cache_v.reshape(dec_batch, depth, past, w_a)
    bias = _window_bias(rpb, dec_seq // GRID_W)

    x, h = _prologue(rows, x_prompt.reshape(m_ctx, d), x_sample.reshape(rows.m_lat, d),
                     g_pre_mix_r, mods, 0)
    new_k = jnp.zeros((batch, depth, seq, w_a), F32)
    new_v = jnp.zeros((batch, depth, seq, w_a), F32)
    for l in range(depth):
        proj = _project(h, w_in, l, col_gates, 1536, F32)
        o_a_ctx, new_k, new_v = _ctx_attention(proj, new_k, new_v, l, depth, batch, seq, w_a)
        o_a_lat = _lat_attention(proj, cache_k_r, cache_v_r, bias, l, m_ctx, dec_batch, dec_seq, w_a)
        o_f_ctx = _fourier(proj, col_f, w_f, batch, seq, 0)
        o_f_lat = _fourier(proj, col_f, w_f, dec_batch, dec_seq, m_ctx // dec_seq)
        o_c = _spatial_gating(proj, col_uv, w_c, g_sgu_row, w_sp_b, b_sp_t, l)
        merged, w1_b, w2_b = _merge(rows, h, o_a_ctx, o_a_lat, o_f_ctx, o_f_lat, o_c, w_gates_b,
                                    w_br_a_b, w_br_f_b, w_br_c_b, w_mlp1, w_mlp2, l)
        x, h2 = _out_projection(rows, merged, w_out_b, x, g_post_mix_r, g_pre_mlp_r, mods, l)
        x, h = _mlp(rows, h2, w1_b, w2_b, x, g_post_mlp_r, g_pre_mix_r, mods, l, l + 1 < depth)

    y_p = x.reshape(batch, seq, d)
    y_s = h.reshape(dec_batch, dec_seq, d)
    cache_shape = (batch, depth, seq, n_heads, HEAD_DIM)
    return (y_p, y_s, new_k.reshape(cache_shape), new_v.reshape(cache_shape))
```

```python
import functools
import math

import numpy as np
import jax
import jax.numpy as jnp
from jax import lax
from jax.experimental import pallas as pl
from jax.experimental.pallas import tpu as pltpu

F32 = jnp.float32
BF16 = jnp.bfloat16

RMS_EPS = 1e-6
MASKED_SCORE = -1e30

HEAD_DIM = 128
GROUP_DIM = 128
GRID_W = 64
CHUNK = 128
MAX_WIN_R = 8
WIN_C = 16
N_MOD = 6
MOD_SHIFT1, MOD_SCALE1, MOD_GATE1, MOD_SHIFT2, MOD_SCALE2, MOD_GATE2 = range(N_MOD)
LANES, SUBLANES = 128, 8
MOD_ROWS = SUBLANES

MIB = 2 ** 20


def _params(semantics, vmem_mib):
    return pltpu.CompilerParams(dimension_semantics=semantics, vmem_limit_bytes=vmem_mib * MIB)


def _rms(x, g):
    return x * lax.rsqrt(jnp.mean(x * x, axis=-1, keepdims=True) + RMS_EPS) * g


def _dot(a, b):
    return jnp.dot(a, b, preferred_element_type=F32)


def _dot_nt(a, b):
    return lax.dot_general(a, b, (((1,), (1,)), ((), ())), preferred_element_type=F32)


def _ada_kernel(c_ref, w_ref, b_ref, o_ref):
    c = c_ref[...]
    s = c * jax.nn.sigmoid(c)
    o_ref[...] = _dot(s.astype(BF16), w_ref[...].astype(BF16)) + b_ref[...]


def _ada(cvec, w_ada, b_ada):
    depth, d, width = w_ada.shape
    tn = 1024
    return pl.pallas_call(
        _ada_kernel,
        name="ada",
        grid=(depth, width // tn),
        in_specs=[
            pl.BlockSpec((MOD_ROWS, d), lambda l, n: (0, 0)),
            pl.BlockSpec((None, d, tn), lambda l, n: (l, 0, n)),
            pl.BlockSpec((None, 1, tn), lambda l, n: (l, 0, n)),
        ],
        out_specs=pl.BlockSpec((None, MOD_ROWS, tn), lambda l, n: (l, 0, n)),
        out_shape=jax.ShapeDtypeStruct((depth, MOD_ROWS, width), F32),
        compiler_params=_params(("parallel", "parallel"), 40),
    )(cvec, w_ada, b_ada.reshape(depth, 1, width))


class _Rows:
    def __init__(self, m_ctx, dec_batch, dec_seq, d):
        self.m_ctx, self.dec_batch, self.dec_seq, self.d = m_ctx, dec_batch, dec_seq, d
        self.m_lat = dec_batch * dec_seq
        self.m = m_ctx + self.m_lat

    def group(self, i, tm):
        n_ctx = self.m_ctx // tm
        per_batch = self.dec_seq // tm
        return jnp.where(i < n_ctx, 0, 1 + (i - n_ctx) // per_batch)

    def mod_spec(self, layer, piece, tm, lag=0):
        return pl.BlockSpec((None, None, None, 1, self.d),
                            lambda i, *_: (layer, piece, self.group(jnp.maximum(i - lag, 0), tm), 0, 0))

    def gain_spec(self, layer):
        return pl.BlockSpec((None, 1, self.d), lambda i, *_: (layer, 0, 0))

    def split_specs(self, tm, width):
        n_ctx = self.m_ctx // tm
        ctx = pl.BlockSpec((tm, width), lambda i, *_: (jnp.minimum(i, n_ctx - 1), 0))
        lat = pl.BlockSpec((tm, width), lambda i, *_: (jnp.maximum(i - n_ctx, 0), 0))
        return ctx, lat

    def pick(self, tm, ctx_ref, lat_ref):
        return jnp.where(pl.program_id(0) < self.m_ctx // tm, ctx_ref[...], lat_ref[...])


def _prologue_kernel(xp_ref, xs_ref, g_ref, sc_ref, sh_ref, x_ref, h_ref, *, rows, tm):
    x = rows.pick(tm, xp_ref, xs_ref)
    x_ref[...] = x
    h = _rms(x, g_ref[...]) * (1.0 + sc_ref[...]) + sh_ref[...]
    h_ref[...] = h.astype(h_ref.dtype)


def _prologue(rows, x_ctx, x_lat, g_pre, mods, layer):
    tm = 512
    row_spec = pl.BlockSpec((tm, rows.d), lambda i: (i, 0))
    return pl.pallas_call(
        functools.partial(_prologue_kernel, rows=rows, tm=tm),
        name="prologue",
        grid=(rows.m // tm,),
        in_specs=[
            *rows.split_specs(tm, rows.d),
            rows.gain_spec(layer),
            rows.mod_spec(layer, MOD_SCALE1, tm),
            rows.mod_spec(layer, MOD_SHIFT1, tm),
        ],
        out_specs=[row_spec, row_spec],
        out_shape=[jax.ShapeDtypeStruct((rows.m, rows.d), F32),
                   jax.ShapeDtypeStruct((rows.m, rows.d), BF16)],
        compiler_params=_params(("arbitrary",), 40),
    )(x_ctx, x_lat, g_pre, mods, mods)


def _inproj_kernel(a_ref, w_ref, o_ref, wb_ref):
    @pl.when(pl.program_id(1) == 0)
    def _():
        wb_ref[...] = w_ref[...].astype(BF16)

    o_ref[...] = _dot(a_ref[...], wb_ref[...]).astype(o_ref.dtype)


def _project(h, w, layer, width, tn, out_dtype):
    m, k = h.shape
    tm = 1024
    assert width % tn == 0
    return pl.pallas_call(
        _inproj_kernel,
        name="in_proj",
        grid=(width // tn, m // tm),
        in_specs=[
            pl.BlockSpec((tm, k), lambda n, i: (i, 0)),
            pl.BlockSpec((None, k, tn), lambda n, i: (layer, 0, n)),
        ],
        out_specs=pl.BlockSpec((tm, tn), lambda n, i: (i, n)),
        out_shape=jax.ShapeDtypeStruct((m, width), out_dtype),
        scratch_shapes=[pltpu.VMEM((k, tn), BF16)],
        compiler_params=_params(("arbitrary", "arbitrary"), 58),
    )(h, w)


def _ctx_attn_kernel(q_ref, k_ref, v_ref, o_ref, ko_ref, vo_ref, *, n_heads, scale):
    ko_ref[...] = k_ref[...]
    vo_ref[...] = v_ref[...]
    for h in range(n_heads):
        cols = slice(h * HEAD_DIM, (h + 1) * HEAD_DIM)
        q = q_ref[:, cols].astype(BF16)
        k = k_ref[:, cols].astype(BF16)
        v = v_ref[:, cols].astype(BF16)
        s = _dot_nt(q, k) * scale
        p = jnp.exp(s - jnp.max(s, axis=-1, keepdims=True))
        denom = jnp.sum(p, axis=-1, keepdims=True)
        o_ref[:, cols] = (_dot(p.astype(BF16), v) / denom).astype(o_ref.dtype)


def _ctx_attention(proj, new_k, new_v, layer, depth, batch, seq, w_a):
    n_heads = w_a // HEAD_DIM
    kern = functools.partial(_ctx_attn_kernel, n_heads=n_heads, scale=1.0 / math.sqrt(HEAD_DIM))
    cache_spec = pl.BlockSpec((None, None, seq, w_a), lambda b: (b, layer, 0, 0))
    cache_shape = jax.ShapeDtypeStruct((batch, depth, seq, w_a), F32)
    in_specs = [pl.BlockSpec((seq, w_a), lambda b, j=j: (b, j)) for j in range(3)]
    in_specs += [pl.BlockSpec(memory_space=pl.ANY)] * 2

    def body(q_ref, k_ref, v_ref, k_all_ref, v_all_ref, o_ref, ko_ref, vo_ref):
        del k_all_ref, v_all_ref
        kern(q_ref, k_ref, v_ref, o_ref, ko_ref, vo_ref)

    return pl.pallas_call(
        body,
        name="ctx_attn",
        grid=(batch,),
        in_specs=in_specs,
        out_specs=[pl.BlockSpec((seq, w_a), lambda b: (b, 0)), cache_spec, cache_spec],
        out_shape=[jax.ShapeDtypeStruct((batch * seq, w_a), BF16), cache_shape, cache_shape],
        input_output_aliases={3: 1, 4: 2},
        compiler_params=_params(("arbitrary",), 32),
    )(proj, proj, proj, new_k, new_v)


def _window_start(r, rows):
    win_r = min(MAX_WIN_R, rows)
    return min(max(r - win_r // 2, 0), rows - win_r)


QUERY_GROUP_ROWS = 4


def _query_groups(rows):
    win_r = min(MAX_WIN_R, rows)
    span = min(rows, win_r + QUERY_GROUP_ROWS)
    assert rows % QUERY_GROUP_ROWS == 0
    groups = []
    for r0 in range(0, rows, QUERY_GROUP_ROWS):
        k0 = min(_window_start(r0, rows), rows - span)
        for r in range(r0, r0 + QUERY_GROUP_ROWS):
            assert k0 <= _window_start(r, rows) and _window_start(r, rows) + win_r <= k0 + span
        groups.append((r0, k0))
    return groups, span


def _bias_kernel(rpb_ref, o_ref, *, rows):
    win_r = min(MAX_WIN_R, rows)
    lanes = rpb_ref.shape[1]
    q = lax.broadcasted_iota(jnp.int32, (GRID_W, lanes), 0)
    kc = lax.broadcasted_iota(jnp.int32, (GRID_W, lanes), 1)
    c_start = jnp.clip(q - WIN_C // 2, 0, GRID_W - WIN_C)
    valid = (kc >= c_start) & (kc < c_start + WIN_C)
    tables = []
    for dr in range(2 * MAX_WIN_R - 1):
        row = jnp.broadcast_to(rpb_ref[dr:dr + 1, :], (GRID_W, lanes))
        t = pltpu.roll(row, lanes - (WIN_C - 1), 1, stride=1, stride_axis=0)
        tables.append(jnp.where(valid, t, MASKED_SCORE)[:, :GRID_W])
    masked = jnp.full((GRID_W, GRID_W), MASKED_SCORE, F32)
    groups, span = _query_groups(rows)
    for g, (r0, k0) in enumerate(groups):
        for j in range(QUERY_GROUP_ROWS):
            start = _window_start(r0 + j, rows)
            for i in range(span):
                in_window = start <= k0 + i < start + win_r
                dr = k0 + i - (r0 + j) + (MAX_WIN_R - 1)
                o_ref[g, j * GRID_W:(j + 1) * GRID_W, i * GRID_W:(i + 1) * GRID_W] = (
                    tables[dr] if in_window else masked)


def _window_bias(rpb, rows):
    depth, n_heads, n_dr, n_dc = rpb.shape
    groups, span = _query_groups(rows)
    block = (len(groups), QUERY_GROUP_ROWS * GRID_W, span * GRID_W)
    dr_pad = -(-n_dr // SUBLANES) * SUBLANES
    rpb_rows = jnp.pad(rpb.reshape(depth * n_heads, n_dr, n_dc),
                       ((0, 0), (0, dr_pad - n_dr), (0, LANES - n_dc)))
    return pl.pallas_call(
        functools.partial(_bias_kernel, rows=rows),
        name="window_bias",
        grid=(depth * n_heads,),
        in_specs=[pl.BlockSpec((None, dr_pad, LANES), lambda i: (i, 0, 0))],
        out_specs=pl.BlockSpec((None,) + block, lambda i: (i, 0, 0, 0)),
        out_shape=jax.ShapeDtypeStruct((depth * n_heads,) + block, F32),
        compiler_params=_params(("parallel",), 32),
    )(rpb_rows)


def _lat_attn_kernel(q_ref, k_ref, v_ref, ck_ref, cv_ref, bias_ref, o_ref, *, rows, scale):
    groups, span = _query_groups(rows)
    ck = ck_ref[...].astype(BF16)
    cv = cv_ref[...].astype(BF16)
    for g, (r0, k0) in enumerate(groups):
        q_rows = slice(r0 * GRID_W, (r0 + QUERY_GROUP_ROWS) * GRID_W)
        k_rows = slice(k0 * GRID_W, (k0 + span) * GRID_W)
        q = q_ref[q_rows, :].astype(BF16)
        kw = k_ref[k_rows, :].astype(BF16)
        vw = v_ref[k_rows, :].astype(BF16)
        s_lat = _dot_nt(q, kw) * scale + bias_ref[g]
        s_ctx = _dot_nt(q, ck) * scale
        top = jnp.maximum(jnp.max(s_lat, axis=-1, keepdims=True),
                          jnp.max(s_ctx, axis=-1, keepdims=True))
        p_lat = jnp.exp(s_lat - top)
        p_ctx = jnp.exp(s_ctx - top)
        denom = jnp.sum(p_lat, axis=-1, keepdims=True) + jnp.sum(p_ctx, axis=-1, keepdims=True)
        o = _dot(p_lat.astype(BF16), vw) + _dot(p_ctx.astype(BF16), cv)
        o_ref[q_rows, :] = (o / denom).astype(o_ref.dtype)


def _lat_attention(proj, cache_k, cache_v, bias, layer, m_ctx, dec_batch, dec_seq, w_a):
    n_heads = w_a // HEAD_DIM
    rows = dec_seq // GRID_W
    past = cache_k.shape[2]
    first = m_ctx // dec_seq
    assert first * dec_seq == m_ctx
    kern = functools.partial(_lat_attn_kernel, rows=rows, scale=1.0 / math.sqrt(HEAD_DIM))
    qkv_specs = [pl.BlockSpec((dec_seq, HEAD_DIM), lambda h, b, j=j: (first + b, j * n_heads + h))
                 for j in range(3)]
    cache_spec = pl.BlockSpec((None, None, past, HEAD_DIM), lambda h, b: (b, layer, 0, h))
    return pl.pallas_call(
        kern,
        name="lat_attn",
        grid=(n_heads, dec_batch),
        in_specs=qkv_specs + [
            cache_spec, cache_spec,
            pl.BlockSpec((None,) + bias.shape[1:], lambda h, b: (layer * n_heads + h, 0, 0, 0)),
        ],
        out_specs=pl.BlockSpec((dec_seq, HEAD_DIM), lambda h, b: (b, h)),
        out_shape=jax.ShapeDtypeStruct((dec_batch * dec_seq, w_a), BF16),
        compiler_params=_params(("parallel", "parallel"), 32),
    )(proj, proj, proj, cache_k, cache_v, bias)


def _dft_tables(n):
    idx = np.arange(n, dtype=np.int64)
    ang = 2.0 * np.pi * ((idx[:, None] * idx[None, :]) % n) / n
    return np.cos(ang), np.sin(ang)


def _fourier_kernel(f_ref, wc_ref, cn_ref, sn_ref, o_ref, *, n_groups, scale):
    wc = wc_ref[...].astype(BF16)
    pc, ps = [], []
    for g in range(n_groups):
        p = _dot(f_ref[:, g * GROUP_DIM:(g + 1) * GROUP_DIM].astype(BF16), wc)
        pc.append(p[:, :GROUP_DIM])
        ps.append(p[:, GROUP_DIM:])
    y = (_dot(cn_ref[...].astype(BF16), jnp.concatenate(pc, axis=1).astype(BF16))
         - _dot(sn_ref[...].astype(BF16), jnp.concatenate(ps, axis=1).astype(BF16)))
    o_ref[...] = (y * scale).astype(o_ref.dtype)


def _fourier(proj, col_start, w_f, n_batch, n_pos, first_block):
    col_block = col_start // w_f
    assert col_block * w_f == col_start
    cc, sc = _dft_tables(GROUP_DIM)
    cn, sn = _dft_tables(n_pos)
    tables = tuple(jnp.asarray(t, F32) for t in (np.concatenate([cc, sc], axis=1), cn, sn))
    const = lambda a: pl.BlockSpec(a.shape, lambda b: (0, 0))
    kern = functools.partial(_fourier_kernel, n_groups=w_f // GROUP_DIM,
                             scale=1.0 / math.sqrt(n_pos * GROUP_DIM))
    return pl.pallas_call(
        kern,
        name="fourier",
        grid=(n_batch,),
        in_specs=[pl.BlockSpec((n_pos, w_f), lambda b: (first_block + b, col_block))]
                 + [const(t) for t in tables],
        out_specs=pl.BlockSpec((n_pos, w_f), lambda b: (b, 0)),
        out_shape=jax.ShapeDtypeStruct((n_batch * n_pos, w_f), BF16),
        compiler_params=_params(("parallel",), 48),
    )(proj, *tables)


def _gelu(x):
    return 0.5 * x * (1.0 + lax.erf(x * math.sqrt(0.5)))


def _sgu_kernel(u_ref, v_ref, g_ref, w_ref, b_ref, o_ref, *, n_groups, n_chunks):
    for g in range(n_groups):
        cols = slice(g * GROUP_DIM, (g + 1) * GROUP_DIM)
        v = _rms(_gelu(v_ref[:, cols]), g_ref[:, cols]).astype(BF16)
        v_wide = jnp.concatenate([v[c * CHUNK:(c + 1) * CHUNK, :] for c in range(n_chunks)], axis=1)
        s = _dot(w_ref[g], v_wide) + b_ref[:, g:g + 1]
        for c in range(n_chunks):
            rws = slice(c * CHUNK, (c + 1) * CHUNK)
            o_ref[rws, cols] = (_gelu(u_ref[rws, cols])
                                * s[:, c * GROUP_DIM:(c + 1) * GROUP_DIM]).astype(o_ref.dtype)


def _spatial_gating(proj, col_start, w_c, g_sgu, w_sp, b_sp_t, layer):
    m = proj.shape[0]
    n_groups = w_c // GROUP_DIM
    n_chunks = 4
    tm = n_chunks * CHUNK
    col_block = col_start // w_c
    assert col_block * w_c == col_start
    kern = functools.partial(_sgu_kernel, n_groups=n_groups, n_chunks=n_chunks)
    return pl.pallas_call(
        kern,
        name="spatial_gate",
        grid=(m // tm,),
        in_specs=[
            pl.BlockSpec((tm, w_c), lambda i: (i, col_block)),
            pl.BlockSpec((tm, w_c), lambda i: (i, col_block + 1)),
            pl.BlockSpec((None, 1, w_c), lambda i: (layer, 0, 0)),
            pl.BlockSpec((None, n_groups, CHUNK, CHUNK), lambda i: (layer, 0, 0, 0)),
            pl.BlockSpec((None, CHUNK, n_groups), lambda i: (layer, 0, 0)),
        ],
        out_specs=pl.BlockSpec((tm, w_c), lambda i: (i, 0)),
        out_shape=jax.ShapeDtypeStruct((m, w_c), BF16),
        compiler_params=_params(("parallel",), 32),
    )(proj, proj, g_sgu, w_sp, b_sp_t)


def _merge_kernel(h_ref, oa_ctx_ref, oa_lat_ref, of_ctx_ref, of_lat_ref, oc_ref,
                  wga_ref, wgf_ref, wgc_ref, wa_ref, wf_ref, wc_ref, w1_ref, w2_ref,
                  o_ref, w1o_ref, w2o_ref, *, rows, tm):
    w1o_ref[...] = w1_ref[...].astype(BF16)
    w2o_ref[...] = w2_ref[...].astype(BF16)
    h = h_ref[...]
    o_a = rows.pick(tm, oa_ctx_ref, oa_lat_ref)
    o_f = rows.pick(tm, of_ctx_ref, of_lat_ref)
    acc = jax.nn.sigmoid(_dot(h, wga_ref[...])) * _dot(o_a, wa_ref[...])
    acc += jax.nn.sigmoid(_dot(h, wgf_ref[...])) * _dot(o_f, wf_ref[...])
    acc += jax.nn.sigmoid(_dot(h, wgc_ref[...])) * _dot(oc_ref[...], wc_ref[...])
    o_ref[...] = acc.astype(o_ref.dtype)


def _merge(rows, h, o_a_ctx, o_a_lat, o_f_ctx, o_f_lat, o_c, w_gates, w_br_a, w_br_f, w_br_c,
           w_mlp1, w_mlp2, layer):
    d = rows.d
    d_ff = w_mlp1.shape[2]
    tm, tn = 1024, 256
    nb = d // tn
    n_steps = (rows.m // tm) * nb
    n_slabs = 1 << (n_steps.bit_length() - 1)
    slab1, slab2 = d // n_slabs, d_ff // n_slabs
    assert slab1 * n_slabs == d and slab2 * n_slabs == d_ff and slab1 % 16 == 0

    def slab(i, n):
        return jnp.minimum(i * nb + n, n_slabs - 1)

    full = lambda width: pl.BlockSpec((tm, width), lambda i, n: (i, 0))
    gate_w = lambda j: pl.BlockSpec((d, tn), lambda i, n: (0, j * nb + n))
    br_w = lambda w: pl.BlockSpec((None, w.shape[1], tn), lambda i, n: (layer, 0, n))
    return pl.pallas_call(
        functools.partial(_merge_kernel, rows=rows, tm=tm),
        name="gate_merge",
        grid=(rows.m // tm, nb),
        in_specs=[full(d),
                  *rows.split_specs(tm, o_a_ctx.shape[1]),
                  *rows.split_specs(tm, o_f_ctx.shape[1]),
                  full(o_c.shape[1]),
                  gate_w(0), gate_w(1), gate_w(2),
                  br_w(w_br_a), br_w(w_br_f), br_w(w_br_c),
                  pl.BlockSpec((None, slab1, d_ff), lambda i, n: (layer, slab(i, n), 0)),
                  pl.BlockSpec((None, slab2, d), lambda i, n: (layer, slab(i, n), 0))],
        out_specs=[pl.BlockSpec((tm, tn), lambda i, n: (i, n)),
                   pl.BlockSpec((slab1, d_ff), lambda i, n: (slab(i, n), 0)),
                   pl.BlockSpec((slab2, d), lambda i, n: (slab(i, n), 0))],
        out_shape=[jax.ShapeDtypeStruct((rows.m, d), BF16),
                   jax.ShapeDtypeStruct((d, d_ff), BF16),
                   jax.ShapeDtypeStruct((d_ff, d), BF16)],
        compiler_params=_params(("arbitrary", "arbitrary"), 56),
    )(h, o_a_ctx, o_a_lat, o_f_ctx, o_f_lat, o_c, w_gates, w_gates, w_gates,
      w_br_a, w_br_f, w_br_c, w_mlp1, w_mlp2)


SUB_ROWS = 256


def _unit_rms(y):
    return y * lax.rsqrt(jnp.mean(y * y, axis=-1, keepdims=True) + RMS_EPS)


def _outproj_kernel(mg_ref, w_ref, x_ref, gpost_ref, gt_ref, gpre_ref, sc_ref, sh_ref,
                    xo_ref, ho_ref, y_even_ref, y_odd_ref, *, n_tiles):
    s = pl.program_id(0)
    y_refs = (y_even_ref, y_odd_ref)

    def multiply(parity):
        y_refs[parity][...] = _dot(mg_ref[...], w_ref[...])

    def finish(parity):
        post = gpost_ref[...] * gt_ref[...]
        pre = gpre_ref[...] * (1.0 + sc_ref[...])
        x1 = x_ref[...] + _unit_rms(y_refs[parity][...]) * post
        xo_ref[...] = x1
        ho_ref[...] = (_unit_rms(x1) * pre + sh_ref[...]).astype(ho_ref.dtype)

    @pl.when(s == 0)
    def _():
        multiply(0)

    for parity in (0, 1):
        @pl.when((s > 0) & (s < n_tiles) & (s % 2 == parity))
        def _():
            multiply(parity)
            finish(1 - parity)

    @pl.when(s == n_tiles)
    def _():
        finish((n_tiles - 1) % 2)


def _out_projection(rows, merged, w_out, x, g_post_mix, g_pre_mlp, mods, layer):
    tm = 512
    d = rows.d
    n_tiles = rows.m // tm
    ahead_spec = pl.BlockSpec((tm, d), lambda s: (jnp.minimum(s, n_tiles - 1), 0))
    lag_spec = pl.BlockSpec((tm, d), lambda s: (jnp.maximum(s - 1, 0), 0))
    return pl.pallas_call(
        functools.partial(_outproj_kernel, n_tiles=n_tiles),
        name="out_proj",
        grid=(n_tiles + 1,),
        in_specs=[
            ahead_spec,
            pl.BlockSpec((d, d), lambda s: (0, 0), pipeline_mode=pl.Buffered(1)),
            lag_spec,
            rows.gain_spec(layer),
            rows.mod_spec(layer, MOD_GATE1, tm, lag=1),
            rows.gain_spec(layer),
            rows.mod_spec(layer, MOD_SCALE2, tm, lag=1),
            rows.mod_spec(layer, MOD_SHIFT2, tm, lag=1),
        ],
        out_specs=[lag_spec, lag_spec],
        out_shape=[jax.ShapeDtypeStruct((rows.m, d), F32), jax.ShapeDtypeStruct((rows.m, d), BF16)],
        scratch_shapes=[pltpu.VMEM((tm, d), F32), pltpu.VMEM((tm, d), F32)],
        compiler_params=_params(("arbitrary",), 52),
    )(merged, w_out, x, g_post_mix, mods, g_pre_mlp, mods, mods)


def _mlp_kernel(*refs, n_ff_steps, has_next, rows, tm):
    if has_next:
        (h_ref, w1_ref, w2_ref, x_ref, gpost_ref, gt_ref, gpre_ref, sc_ref, sh_ref,
         wg_ref, wo_ref, xo_ref, ho_ref, wgo_ref, woo_ref, acc_ref) = refs
        wgo_ref[...] = wg_ref[...].astype(BF16)
        woo_ref[...] = wo_ref[...].astype(BF16)
    else:
        h_ref, w1_ref, w2_ref, x_ref, gpost_ref, gt_ref, yp_ref, ys_ref, acc_ref = refs
    step = pl.program_id(1)
    last = n_ff_steps - 1
    assert last >= 2

    def ff_slice(rws):
        hid = jnp.square(jnp.maximum(_dot(h_ref[rws, :], w1_ref[...]), 0.0)).astype(BF16)
        return _dot(hid, w2_ref[...])

    whole = slice(None)

    @pl.when(step == 0)
    def _():
        acc_ref[...] = ff_slice(whole)

    @pl.when((step > 0) & (step < last))
    def _():
        acc_ref[...] += ff_slice(whole)

    @pl.when(step == last)
    def _():
        post = gpost_ref[...] * gt_ref[...]
        if has_next:
            pre = gpre_ref[...] * (1.0 + sc_ref[...])
            shift = sh_ref[...]
        for c in range(tm // SUB_ROWS):
            rws = slice(c * SUB_ROWS, (c + 1) * SUB_ROWS)
            x2 = x_ref[rws, :] + _unit_rms(acc_ref[rws, :] + ff_slice(rws)) * post
            if has_next:
                xo_ref[rws, :] = x2
                ho_ref[rws, :] = (_unit_rms(x2) * pre + shift).astype(ho_ref.dtype)
            else:
                acc_ref[rws, :] = x2
        if not has_next:
            is_ctx = pl.program_id(0) < rows.m_ctx // tm

            @pl.when(is_ctx)
            def _():
                yp_ref[...] = acc_ref[...]

            @pl.when(jnp.logical_not(is_ctx))
            def _():
                ys_ref[...] = acc_ref[...]


def _mlp(rows, h2, w1, w2, x, g_post_mlp, g_pre_next, mods, w_in, col_gates, w_out, layer, has_next):
    tm, tf = 512, 1024
    d = rows.d
    d_ff = w1.shape[1]
    n_ff_steps = d_ff // tf
    row_spec = pl.BlockSpec((tm, d), lambda i, f: (i, 0))
    in_specs = [
        row_spec,
        pl.BlockSpec((d, tf), lambda i, f: (0, f)),
        pl.BlockSpec((tf, d), lambda i, f: (f, 0)),
        row_spec,
        rows.gain_spec(layer),
        rows.mod_spec(layer, MOD_GATE2, tm),
    ]
    args = [h2, w1, w2, x, g_post_mlp, mods]
    if has_next:
        in_specs += [rows.gain_spec(layer + 1),
                     rows.mod_spec(layer + 1, MOD_SCALE1, tm),
                     rows.mod_spec(layer + 1, MOD_SHIFT1, tm)]
        n_steps = (rows.m // tm) * n_ff_steps
        n_slabs = 1 << (n_steps.bit_length() - 1)
        gate_width = w_in.shape[2] - col_gates
        piece = math.gcd(col_gates, gate_width)
        n_pieces = gate_width // piece
        g_rows, o_rows = d * n_pieces // n_slabs, d // n_slabs
        assert g_rows % 16 == 0 and o_rows % 16 == 0 and g_rows * n_slabs == d * n_pieces

        def slab(i, f):
            return jnp.minimum(i * n_ff_steps + f, n_slabs - 1)

        in_specs += [
            pl.BlockSpec((None, g_rows, piece),
                         lambda i, f: (layer + 1, slab(i, f) // n_pieces,
                                       col_gates // piece + slab(i, f) % n_pieces)),
            pl.BlockSpec((None, o_rows, d), lambda i, f: (layer + 1, slab(i, f), 0)),
        ]
        args += [g_pre_next, mods, mods, w_in, w_out]
        out_specs = [row_spec, row_spec,
                     pl.BlockSpec((g_rows, piece),
                                  lambda i, f: (slab(i, f) // n_pieces, slab(i, f) % n_pieces)),
                     pl.BlockSpec((o_rows, d), lambda i, f: (slab(i, f), 0))]
        out_shape = [jax.ShapeDtypeStruct((rows.m, d), F32),
                     jax.ShapeDtypeStruct((rows.m, d), BF16),
                     jax.ShapeDtypeStruct((d, gate_width), BF16),
                     jax.ShapeDtypeStruct((d, d), BF16)]
    else:
        out_specs = list(rows.split_specs(tm, d))
        out_shape = [jax.ShapeDtypeStruct((rows.m_ctx, d), F32),
                     jax.ShapeDtypeStruct((rows.m_lat, d), F32)]
    return pl.pallas_call(
        functools.partial(_mlp_kernel, n_ff_steps=n_ff_steps, has_next=has_next, rows=rows, tm=tm),
        name="mlp",
        grid=(rows.m // tm, n_ff_steps),
        in_specs=in_specs,
        out_specs=out_specs,
        out_shape=out_shape,
        scratch_shapes=[pltpu.VMEM((tm, d), F32)],
        compiler_params=_params(("arbitrary", "arbitrary"), 56),
    )(*args)


def kernel(x_prompt, x_sample, cache_k, cache_v, c, c_ctx, w_ada, b_ada, g_pre_mix, g_post_mix,
           g_pre_mlp, g_post_mlp, w_in, rpb, g_sgu, w_spatial, b_spatial, w_br_a, w_br_f, w_br_c,
           w_out, w_mlp1, w_mlp2):
    batch, seq, d = x_prompt.shape
    dec_batch, dec_seq, _ = x_sample.shape
    depth = w_in.shape[0]
    past = cache_k.shape[2]
    w_a = w_br_a.shape[1]
    w_f = w_br_f.shape[1]
    w_c = w_br_c.shape[1]
    n_heads = w_a // HEAD_DIM
    m_ctx = batch * seq
    rows = _Rows(m_ctx, dec_batch, dec_seq, d)
    assert dec_seq % GRID_W == 0 and m_ctx % dec_seq == 0 and MOD_ROWS >= 1 + dec_batch

    cvec = jnp.zeros((MOD_ROWS, d), F32).at[0].set(c_ctx).at[1:1 + dec_batch].set(c)
    mods = _ada(cvec, w_ada, b_ada)
    mods = mods[:, :1 + dec_batch].reshape(depth, 1 + dec_batch, N_MOD, d)
    mods = mods.transpose(0, 2, 1, 3)[:, :, :, None, :]

    col_f = 3 * w_a
    col_uv = col_f + w_f
    col_gates = col_uv + 2 * w_c

    w_gates_b = w_in[0, :, col_gates:].astype(BF16)
    w_out_b = w_out[0].astype(BF16)
    w_br_a_b, w_br_f_b, w_br_c_b = w_br_a.astype(BF16), w_br_f.astype(BF16), w_br_c.astype(BF16)
    w_sp_b = w_spatial.astype(BF16)
    b_sp_t = b_spatial.transpose(0, 2, 1)
    g_sgu_row = g_sgu.reshape(depth, 1, w_c)
    gains = [g.reshape(depth, 1, d) for g in (g_pre_mix, g_post_mix, g_pre_mlp, g_post_mlp)]
    g_pre_mix_r, g_post_mix_r, g_pre_mlp_r, g_post_mlp_r = gains
    cache_k_r = cache_k.reshape(dec_batch, depth, past, w_a)
    cache_v_r = cache_v.reshape(dec_batch, depth, past, w_a)
    bias = _window_bias(rpb, dec_seq // GRID_W)

    x, h = _prologue(rows, x_prompt.reshape(m_ctx, d), x_sample.reshape(rows.m_lat, d),
                     g_pre_mix_r, mods, 0)
    new_k = jnp.zeros((batch, depth, seq, w_a), F32)
    new_v = jnp.zeros((batch, depth, seq, w_a), F32)
    for l in range(depth):
        proj = _project(h, w_in, l, col_gates, 1536, F32)
        o_a_ctx, new_k, new_v = _ctx_attention(proj, new_k, new_v, l, depth, batch, seq, w_a)
        o_a_lat = _lat_attention(proj, cache_k_r, cache_v_r, bias, l, m_ctx, dec_batch, dec_seq, w_a)
        o_f_ctx = _fourier(proj, col_f, w_f, batch, seq, 0)
        o_f_lat = _fourier(proj, col_f, w_f, dec_batch, dec_seq, m_ctx // dec_seq)
        o_c = _spatial_gating(proj, col_uv, w_c, g_sgu_row, w_sp_b, b_sp_t, l)
        merged, w1_b, w2_b = _merge(rows, h, o_a_ctx, o_a_lat, o_f_ctx, o_f_lat, o_c, w_gates_b,
                                    w_br_a_b, w_br_f_b, w_br_c_b, w_mlp1, w_mlp2, l)
        x, h2 = _out_projection(rows, merged, w_out_b, x, g_post_mix_r, g_pre_mlp_r, mods, l)
        if l + 1 == depth:
            new_k, new_v, x, h2 = lax.optimization_barrier((new_k, new_v, x, h2))
        outs = _mlp(rows, h2, w1_b, w2_b, x, g_post_mlp_r, g_pre_mix_r, mods, w_in, col_gates,
                    w_out, l, l + 1 < depth)
        if l + 1 < depth:
            x, h, w_gates_b, w_out_b = outs

    y_p = outs[0].reshape(batch, seq, d)
    y_s = outs[1].reshape(dec_batch, dec_seq, d)
    cache_shape = (batch, depth, seq, n_heads, HEAD_DIM)
    return (y_p, y_s, new_k.reshape(cache_shape), new_v.reshape(cache_shape))
```

```python
import functools
import math

import numpy as np
import jax
import jax.numpy as jnp
from jax import lax
from jax.experimental import pallas as pl
from jax.experimental.pallas import tpu as pltpu

F32 = jnp.float32
BF16 = jnp.bfloat16

RMS_EPS = 1e-6
MASKED_SCORE = -1e30

HEAD_DIM = 128
GROUP_DIM = 128
GRID_W = 64
CHUNK = 128
MAX_WIN_R = 8
WIN_C = 16
N_MOD = 6
MOD_SHIFT1, MOD_SCALE1, MOD_GATE1, MOD_SHIFT2, MOD_SCALE2, MOD_GATE2 = range(N_MOD)
LANES, SUBLANES = 128, 8
MOD_ROWS = SUBLANES

MIB = 2 ** 20


def _params(semantics, vmem_mib):
    return pltpu.CompilerParams(dimension_semantics=semantics, vmem_limit_bytes=vmem_mib * MIB)


def _rms(x, g):
    return x * lax.rsqrt(jnp.mean(x * x, axis=-1, keepdims=True) + RMS_EPS) * g


def _dot(a, b):
    return jnp.dot(a, b, preferred_element_type=F32)


def _dot_nt(a, b):
    return lax.dot_general(a, b, (((1,), (1,)), ((), ())), preferred_element_type=F32)


def _ada_kernel(c_ref, w_ref, b_ref, o_ref):
    c = c_ref[...]
    s = c * jax.nn.sigmoid(c)
    o_ref[...] = _dot(s.astype(BF16), w_ref[...].astype(BF16)) + b_ref[...]


def _ada(cvec, w_ada, b_ada):
    depth, d, width = w_ada.shape
    tn = 1024
    return pl.pallas_call(
        _ada_kernel,
        name="ada",
        grid=(depth, width // tn),
        in_specs=[
            pl.BlockSpec((MOD_ROWS, d), lambda l, n: (0, 0)),
            pl.BlockSpec((None, d, tn), lambda l, n: (l, 0, n)),
            pl.BlockSpec((None, 1, tn), lambda l, n: (l, 0, n)),
        ],
        out_specs=pl.BlockSpec((None, MOD_ROWS, tn), lambda l, n: (l, 0, n)),
        out_shape=jax.ShapeDtypeStruct((depth, MOD_ROWS, width), F32),
        compiler_params=_params(("parallel", "parallel"), 40),
    )(cvec, w_ada, b_ada.reshape(depth, 1, width))


class _Rows:
    def __init__(self, m_ctx, dec_batch, dec_seq, d):
        self.m_ctx, self.dec_batch, self.dec_seq, self.d = m_ctx, dec_batch, dec_seq, d
        self.m_lat = dec_batch * dec_seq
        self.m = m_ctx + self.m_lat

    def group(self, i, tm):
        n_ctx = self.m_ctx // tm
        per_batch = self.dec_seq // tm
        return jnp.where(i < n_ctx, 0, 1 + (i - n_ctx) // per_batch)

    def mod_spec(self, layer, piece, tm, lag=0):
        return pl.BlockSpec((None, None, None, 1, self.d),
                            lambda i, *_: (layer, piece, self.group(jnp.maximum(i - lag, 0), tm), 0, 0))

    def gain_spec(self, layer):
        return pl.BlockSpec((None, 1, self.d), lambda i, *_: (layer, 0, 0))

    def split_specs(self, tm, width):
        n_ctx = self.m_ctx // tm
        ctx = pl.BlockSpec((tm, width), lambda i, *_: (jnp.minimum(i, n_ctx - 1), 0))
        lat = pl.BlockSpec((tm, width), lambda i, *_: (jnp.maximum(i - n_ctx, 0), 0))
        return ctx, lat

    def pick(self, tm, ctx_ref, lat_ref):
        return jnp.where(pl.program_id(0) < self.m_ctx // tm, ctx_ref[...], lat_ref[...])


def _prologue_kernel(xp_ref, xs_ref, g_ref, sc_ref, sh_ref, x_ref, h_ref, *, rows, tm):
    x = rows.pick(tm, xp_ref, xs_ref)
    x_ref[...] = x
    h = _rms(x, g_ref[...]) * (1.0 + sc_ref[...]) + sh_ref[...]
    h_ref[...] = h.astype(h_ref.dtype)


def _prologue(rows, x_ctx, x_lat, g_pre, mods, layer):
    tm = 512
    row_spec = pl.BlockSpec((tm, rows.d), lambda i: (i, 0))
    return pl.pallas_call(
        functools.partial(_prologue_kernel, rows=rows, tm=tm),
        name="prologue",
        grid=(rows.m // tm,),
        in_specs=[
            *rows.split_specs(tm, rows.d),
            rows.gain_spec(layer),
            rows.mod_spec(layer, MOD_SCALE1, tm),
            rows.mod_spec(layer, MOD_SHIFT1, tm),
        ],
        out_specs=[row_spec, row_spec],
        out_shape=[jax.ShapeDtypeStruct((rows.m, rows.d), F32),
                   jax.ShapeDtypeStruct((rows.m, rows.d), BF16)],
        compiler_params=_params(("arbitrary",), 40),
    )(x_ctx, x_lat, g_pre, mods, mods)


def _inproj_kernel(a_ref, w_ref, o_ref, wb_ref):
    @pl.when(pl.program_id(1) == 0)
    def _():
        wb_ref[...] = w_ref[...].astype(BF16)

    o_ref[...] = _dot(a_ref[...], wb_ref[...]).astype(o_ref.dtype)


def _project(h, w, layer, width, tn, out_dtype):
    m, k = h.shape
    tm = 1024
    assert width % tn == 0
    return pl.pallas_call(
        _inproj_kernel,
        name="in_proj",
        grid=(width // tn, m // tm),
        in_specs=[
            pl.BlockSpec((tm, k), lambda n, i: (i, 0)),
            pl.BlockSpec((None, k, tn), lambda n, i: (layer, 0, n)),
        ],
        out_specs=pl.BlockSpec((tm, tn), lambda n, i: (i, n)),
        out_shape=jax.ShapeDtypeStruct((m, width), out_dtype),
        scratch_shapes=[pltpu.VMEM((k, tn), BF16)],
        compiler_params=_params(("arbitrary", "arbitrary"), 58),
    )(h, w)


def _ctx_attn_kernel(q_ref, k_ref, v_ref, o_ref, ko_ref, vo_ref, *, n_heads, scale):
    ko_ref[...] = k_ref[...].reshape(ko_ref.shape)
    vo_ref[...] = v_ref[...].reshape(vo_ref.shape)
    for h in range(n_heads):
        cols = slice(h * HEAD_DIM, (h + 1) * HEAD_DIM)
        q = q_ref[:, cols].astype(BF16)
        k = k_ref[:, cols].astype(BF16)
        v = v_ref[:, cols].astype(BF16)
        s = _dot_nt(q, k) * scale
        p = jnp.exp(s - jnp.max(s, axis=-1, keepdims=True))
        denom = jnp.sum(p, axis=-1, keepdims=True)
        o_ref[:, cols] = (_dot(p.astype(BF16), v) / denom).astype(o_ref.dtype)


def _ctx_attention(proj, new_k, new_v, layer, depth, batch, seq, w_a):
    n_heads = w_a // HEAD_DIM
    kern = functools.partial(_ctx_attn_kernel, n_heads=n_heads, scale=1.0 / math.sqrt(HEAD_DIM))
    cache_spec = pl.BlockSpec((None, None, seq, n_heads, HEAD_DIM), lambda b: (b, layer, 0, 0, 0))
    cache_shape = jax.ShapeDtypeStruct((batch, depth, seq, n_heads, HEAD_DIM), F32)
    in_specs = [pl.BlockSpec((seq, w_a), lambda b, j=j: (b, j)) for j in range(3)]
    in_specs += [pl.BlockSpec(memory_space=pl.ANY)] * 2

    def body(q_ref, k_ref, v_ref, k_all_ref, v_all_ref, o_ref, ko_ref, vo_ref):
        del k_all_ref, v_all_ref
        kern(q_ref, k_ref, v_ref, o_ref, ko_ref, vo_ref)

    return pl.pallas_call(
        body,
        name="ctx_attn",
        grid=(batch,),
        in_specs=in_specs,
        out_specs=[pl.BlockSpec((seq, w_a), lambda b: (b, 0)), cache_spec, cache_spec],
        out_shape=[jax.ShapeDtypeStruct((batch * seq, w_a), BF16), cache_shape, cache_shape],
        input_output_aliases={3: 1, 4: 2},
        compiler_params=_params(("arbitrary",), 32),
    )(proj, proj, proj, new_k, new_v)


def _window_start(r, rows):
    win_r = min(MAX_WIN_R, rows)
    return min(max(r - win_r // 2, 0), rows - win_r)


QUERY_GROUP_ROWS = 4


def _query_groups(rows):
    win_r = min(MAX_WIN_R, rows)
    span = min(rows, win_r + QUERY_GROUP_ROWS)
    assert rows % QUERY_GROUP_ROWS == 0
    groups = []
    for r0 in range(0, rows, QUERY_GROUP_ROWS):
        k0 = min(_window_start(r0, rows), rows - span)
        for r in range(r0, r0 + QUERY_GROUP_ROWS):
            assert k0 <= _window_start(r, rows) and _window_start(r, rows) + win_r <= k0 + span
        groups.append((r0, k0))
    return groups, span


def _bias_kernel(rpb_ref, o_ref, *, rows):
    win_r = min(MAX_WIN_R, rows)
    lanes = rpb_ref.shape[1]
    q = lax.broadcasted_iota(jnp.int32, (GRID_W, lanes), 0)
    kc = lax.broadcasted_iota(jnp.int32, (GRID_W, lanes), 1)
    c_start = jnp.clip(q - WIN_C // 2, 0, GRID_W - WIN_C)
    valid = (kc >= c_start) & (kc < c_start + WIN_C)
    tables = []
    for dr in range(2 * MAX_WIN_R - 1):
        row = jnp.broadcast_to(rpb_ref[dr:dr + 1, :], (GRID_W, lanes))
        t = pltpu.roll(row, lanes - (WIN_C - 1), 1, stride=1, stride_axis=0)
        tables.append(jnp.where(valid, t, MASKED_SCORE)[:, :GRID_W])
    masked = jnp.full((GRID_W, GRID_W), MASKED_SCORE, F32)
    groups, span = _query_groups(rows)
    for g, (r0, k0) in enumerate(groups):
        for j in range(QUERY_GROUP_ROWS):
            start = _window_start(r0 + j, rows)
            for i in range(span):
                in_window = start <= k0 + i < start + win_r
                dr = k0 + i - (r0 + j) + (MAX_WIN_R - 1)
                o_ref[g, j * GRID_W:(j + 1) * GRID_W, i * GRID_W:(i + 1) * GRID_W] = (
                    tables[dr] if in_window else masked)


def _window_bias(rpb, rows):
    depth, n_heads, n_dr, n_dc = rpb.shape
    groups, span = _query_groups(rows)
    block = (len(groups), QUERY_GROUP_ROWS * GRID_W, span * GRID_W)
    dr_pad = -(-n_dr // SUBLANES) * SUBLANES
    rpb_rows = jnp.pad(rpb.reshape(depth * n_heads, n_dr, n_dc),
                       ((0, 0), (0, dr_pad - n_dr), (0, LANES - n_dc)))
    return pl.pallas_call(
        functools.partial(_bias_kernel, rows=rows),
        name="window_bias",
        grid=(depth * n_heads,),
        in_specs=[pl.BlockSpec((None, dr_pad, LANES), lambda i: (i, 0, 0))],
        out_specs=pl.BlockSpec((None,) + block, lambda i: (i, 0, 0, 0)),
        out_shape=jax.ShapeDtypeStruct((depth * n_heads,) + block, F32),
        compiler_params=_params(("parallel",), 32),
    )(rpb_rows)


def _lat_attn_kernel(q_ref, k_ref, v_ref, ck_ref, cv_ref, bias_ref, o_ref, *, rows, scale):
    groups, span = _query_groups(rows)
    ck = ck_ref[...].astype(BF16)
    cv = cv_ref[...].astype(BF16)
    for g, (r0, k0) in enumerate(groups):
        q_rows = slice(r0 * GRID_W, (r0 + QUERY_GROUP_ROWS) * GRID_W)
        k_rows = slice(k0 * GRID_W, (k0 + span) * GRID_W)
        q = q_ref[q_rows, :].astype(BF16)
        kw = k_ref[k_rows, :].astype(BF16)
        vw = v_ref[k_rows, :].astype(BF16)
        s_lat = _dot_nt(q, kw) * scale + bias_ref[g]
        s_ctx = _dot_nt(q, ck) * scale
        top = jnp.maximum(jnp.max(s_lat, axis=-1, keepdims=True),
                          jnp.max(s_ctx, axis=-1, keepdims=True))
        p_lat = jnp.exp(s_lat - top)
        p_ctx = jnp.exp(s_ctx - top)
        denom = jnp.sum(p_lat, axis=-1, keepdims=True) + jnp.sum(p_ctx, axis=-1, keepdims=True)
        o = _dot(p_lat.astype(BF16), vw) + _dot(p_ctx.astype(BF16), cv)
        o_ref[q_rows, :] = (o / denom).astype(o_ref.dtype)


def _lat_attention(proj, cache_k, cache_v, bias, layer, m_ctx, dec_batch, dec_seq, w_a):
    n_heads = w_a // HEAD_DIM
    rows = dec_seq // GRID_W
    past = cache_k.shape[2]
    first = m_ctx // dec_seq
    assert first * dec_seq == m_ctx
    kern = functools.partial(_lat_attn_kernel, rows=rows, scale=1.0 / math.sqrt(HEAD_DIM))
    qkv_specs = [pl.BlockSpec((dec_seq, HEAD_DIM), lambda h, b, j=j: (first + b, j * n_heads + h))
                 for j in range(3)]
    cache_spec = pl.BlockSpec((None, None, past, HEAD_DIM), lambda h, b: (b, layer, 0, h))
    return pl.pallas_call(
        kern,
        name="lat_attn",
        grid=(n_heads, dec_batch),
        in_specs=qkv_specs + [
            cache_spec, cache_spec,
            pl.BlockSpec((None,) + bias.shape[1:], lambda h, b: (layer * n_heads + h, 0, 0, 0)),
        ],
        out_specs=pl.BlockSpec((dec_seq, HEAD_DIM), lambda h, b: (b, h)),
        out_shape=jax.ShapeDtypeStruct((dec_batch * dec_seq, w_a), BF16),
        compiler_params=_params(("parallel", "parallel"), 32),
    )(proj, proj, proj, cache_k, cache_v, bias)


def _dft_tables(n):
    idx = np.arange(n, dtype=np.int64)
    ang = 2.0 * np.pi * ((idx[:, None] * idx[None, :]) % n) / n
    return np.cos(ang), np.sin(ang)


def _fourier_kernel(f_ref, wc_ref, cn_ref, sn_ref, o_ref, *, n_groups, scale):
    wc = wc_ref[...].astype(BF16)
    pc, ps = [], []
    for g in range(n_groups):
        p = _dot(f_ref[:, g * GROUP_DIM:(g + 1) * GROUP_DIM].astype(BF16), wc)
        pc.append(p[:, :GROUP_DIM])
        ps.append(p[:, GROUP_DIM:])
    y = (_dot(cn_ref[...].astype(BF16), jnp.concatenate(pc, axis=1).astype(BF16))
         - _dot(sn_ref[...].astype(BF16), jnp.concatenate(ps, axis=1).astype(BF16)))
    o_ref[...] = (y * scale).astype(o_ref.dtype)


def _fourier(proj, col_start, w_f, n_batch, n_pos, first_block):
    col_block = col_start // w_f
    assert col_block * w_f == col_start
    cc, sc = _dft_tables(GROUP_DIM)
    cn, sn = _dft_tables(n_pos)
    tables = tuple(jnp.asarray(t, F32) for t in (np.concatenate([cc, sc], axis=1), cn, sn))
    const = lambda a: pl.BlockSpec(a.shape, lambda b: (0, 0))
    kern = functools.partial(_fourier_kernel, n_groups=w_f // GROUP_DIM,
                             scale=1.0 / math.sqrt(n_pos * GROUP_DIM))
    return pl.pallas_call(
        kern,
        name="fourier",
        grid=(n_batch,),
        in_specs=[pl.BlockSpec((n_pos, w_f), lambda b: (first_block + b, col_block))]
                 + [const(t) for t in tables],
        out_specs=pl.BlockSpec((n_pos, w_f), lambda b: (b, 0)),
        out_shape=jax.ShapeDtypeStruct((n_batch * n_pos, w_f), BF16),
        compiler_params=_params(("parallel",), 48),
    )(proj, *tables)


def _gelu(x):
    return 0.5 * x * (1.0 + lax.erf(x * math.sqrt(0.5)))


def _sgu_kernel(u_ref, v_ref, g_ref, w_ref, b_ref, o_ref, *, n_groups, n_chunks):
    for g in range(n_groups):
        cols = slice(g * GROUP_DIM, (g + 1) * GROUP_DIM)
        v = _rms(_gelu(v_ref[:, cols]), g_ref[:, cols]).astype(BF16)
        v_wide = jnp.concatenate([v[c * CHUNK:(c + 1) * CHUNK, :] for c in range(n_chunks)], axis=1)
        s = _dot(w_ref[g], v_wide) + b_ref[:, g:g + 1]
        for c in range(n_chunks):
            rws = slice(c * CHUNK, (c + 1) * CHUNK)
            o_ref[rws, cols] = (_gelu(u_ref[rws, cols])
                                * s[:, c * GROUP_DIM:(c + 1) * GROUP_DIM]).astype(o_ref.dtype)


def _spatial_gating(proj, col_start, w_c, g_sgu, w_sp, b_sp_t, layer):
    m = proj.shape[0]
    n_groups = w_c // GROUP_DIM
    n_chunks = 4
    tm = n_chunks * CHUNK
    col_block = col_start // w_c
    assert col_block * w_c == col_start
    kern = functools.partial(_sgu_kernel, n_groups=n_groups, n_chunks=n_chunks)
    return pl.pallas_call(
        kern,
        name="spatial_gate",
        grid=(m // tm,),
        in_specs=[
            pl.BlockSpec((tm, w_c), lambda i: (i, col_block)),
            pl.BlockSpec((tm, w_c), lambda i: (i, col_block + 1)),
            pl.BlockSpec((None, 1, w_c), lambda i: (layer, 0, 0)),
            pl.BlockSpec((None, n_groups, CHUNK, CHUNK), lambda i: (layer, 0, 0, 0)),
            pl.BlockSpec((None, CHUNK, n_groups), lambda i: (layer, 0, 0)),
        ],
        out_specs=pl.BlockSpec((tm, w_c), lambda i: (i, 0)),
        out_shape=jax.ShapeDtypeStruct((m, w_c), BF16),
        compiler_params=_params(("parallel",), 32),
    )(proj, proj, g_sgu, w_sp, b_sp_t)


def _merge_kernel(h_ref, oa_ctx_ref, oa_lat_ref, of_ctx_ref, of_lat_ref, oc_ref,
                  wga_ref, wgf_ref, wgc_ref, wa_ref, wf_ref, wc_ref, w1_ref, w2_ref,
                  o_ref, w1o_ref, w2o_ref, *, rows, tm):
    w1o_ref[...] = w1_ref[...].astype(BF16)
    w2o_ref[...] = w2_ref[...].astype(BF16)
    h = h_ref[...]
    o_a = rows.pick(tm, oa_ctx_ref, oa_lat_ref)
    o_f = rows.pick(tm, of_ctx_ref, of_lat_ref)
    acc = jax.nn.sigmoid(_dot(h, wga_ref[...])) * _dot(o_a, wa_ref[...])
    acc += jax.nn.sigmoid(_dot(h, wgf_ref[...])) * _dot(o_f, wf_ref[...])
    acc += jax.nn.sigmoid(_dot(h, wgc_ref[...])) * _dot(oc_ref[...], wc_ref[...])
    o_ref[...] = acc.astype(o_ref.dtype)


def _merge(rows, h, o_a_ctx, o_a_lat, o_f_ctx, o_f_lat, o_c, w_gates, w_br_a, w_br_f, w_br_c,
           w_mlp1, w_mlp2, layer):
    d = rows.d
    d_ff = w_mlp1.shape[2]
    tm, tn = 1024, 256
    nb = d // tn
    n_steps = (rows.m // tm) * nb
    n_slabs = 1 << (n_steps.bit_length() - 1)
    slab1, slab2 = d // n_slabs, d_ff // n_slabs
    assert slab1 * n_slabs == d and slab2 * n_slabs == d_ff and slab1 % 16 == 0

    def slab(i, n):
        return jnp.minimum(i * nb + n, n_slabs - 1)

    full = lambda width: pl.BlockSpec((tm, width), lambda i, n: (i, 0))
    gate_w = lambda j: pl.BlockSpec((d, tn), lambda i, n: (0, j * nb + n))
    br_w = lambda w: pl.BlockSpec((None, w.shape[1], tn), lambda i, n: (layer, 0, n))
    return pl.pallas_call(
        functools.partial(_merge_kernel, rows=rows, tm=tm),
        name="gate_merge",
        grid=(rows.m // tm, nb),
        in_specs=[full(d),
                  *rows.split_specs(tm, o_a_ctx.shape[1]),
                  *rows.split_specs(tm, o_f_ctx.shape[1]),
                  full(o_c.shape[1]),
                  gate_w(0), gate_w(1), gate_w(2),
                  br_w(w_br_a), br_w(w_br_f), br_w(w_br_c),
                  pl.BlockSpec((None, slab1, d_ff), lambda i, n: (layer, slab(i, n), 0)),
                  pl.BlockSpec((None, slab2, d), lambda i, n: (layer, slab(i, n), 0))],
        out_specs=[pl.BlockSpec((tm, tn), lambda i, n: (i, n)),
                   pl.BlockSpec((slab1, d_ff), lambda i, n: (slab(i, n), 0)),
                   pl.BlockSpec((slab2, d), lambda i, n: (slab(i, n), 0))],
        out_shape=[jax.ShapeDtypeStruct((rows.m, d), BF16),
                   jax.ShapeDtypeStruct((d, d_ff), BF16),
                   jax.ShapeDtypeStruct((d_ff, d), BF16)],
        compiler_params=_params(("arbitrary", "arbitrary"), 56),
    )(h, o_a_ctx, o_a_lat, o_f_ctx, o_f_lat, o_c, w_gates, w_gates, w_gates,
      w_br_a, w_br_f, w_br_c, w_mlp1, w_mlp2)


SUB_ROWS = 256


def _unit_rms(y):
    return y * lax.rsqrt(jnp.mean(y * y, axis=-1, keepdims=True) + RMS_EPS)


def _outproj_kernel(mg_ref, w_ref, x_ref, gpost_ref, gt_ref, gpre_ref, sc_ref, sh_ref,
                    xo_ref, ho_ref, y_even_ref, y_odd_ref, *, n_tiles):
    s = pl.program_id(0)
    y_refs = (y_even_ref, y_odd_ref)

    def multiply(parity):
        y_refs[parity][...] = _dot(mg_ref[...], w_ref[...])

    def finish(parity):
        post = gpost_ref[...] * gt_ref[...]
        pre = gpre_ref[...] * (1.0 + sc_ref[...])
        x1 = x_ref[...] + _unit_rms(y_refs[parity][...]) * post
        xo_ref[...] = x1
        ho_ref[...] = (_unit_rms(x1) * pre + sh_ref[...]).astype(ho_ref.dtype)

    @pl.when(s == 0)
    def _():
        multiply(0)

    for parity in (0, 1):
        @pl.when((s > 0) & (s < n_tiles) & (s % 2 == parity))
        def _():
            multiply(parity)
            finish(1 - parity)

    @pl.when(s == n_tiles)
    def _():
        finish((n_tiles - 1) % 2)


def _out_projection(rows, merged, w_out, x, g_post_mix, g_pre_mlp, mods, layer):
    tm = 512
    d = rows.d
    n_tiles = rows.m // tm
    ahead_spec = pl.BlockSpec((tm, d), lambda s: (jnp.minimum(s, n_tiles - 1), 0))
    lag_spec = pl.BlockSpec((tm, d), lambda s: (jnp.maximum(s - 1, 0), 0))
    return pl.pallas_call(
        functools.partial(_outproj_kernel, n_tiles=n_tiles),
        name="out_proj",
        grid=(n_tiles + 1,),
        in_specs=[
            ahead_spec,
            pl.BlockSpec((d, d), lambda s: (0, 0), pipeline_mode=pl.Buffered(1)),
            lag_spec,
            rows.gain_spec(layer),
            rows.mod_spec(layer, MOD_GATE1, tm, lag=1),
            rows.gain_spec(layer),
            rows.mod_spec(layer, MOD_SCALE2, tm, lag=1),
            rows.mod_spec(layer, MOD_SHIFT2, tm, lag=1),
        ],
        out_specs=[lag_spec, lag_spec],
        out_shape=[jax.ShapeDtypeStruct((rows.m, d), F32), jax.ShapeDtypeStruct((rows.m, d), BF16)],
        scratch_shapes=[pltpu.VMEM((tm, d), F32), pltpu.VMEM((tm, d), F32)],
        compiler_params=_params(("arbitrary",), 52),
    )(merged, w_out, x, g_post_mix, mods, g_pre_mlp, mods, mods)


def _mlp_kernel(*refs, n_ff_steps, has_next, rows, tm):
    if has_next:
        (h_ref, w1_ref, w2_ref, x_ref, gpost_ref, gt_ref, gpre_ref, sc_ref, sh_ref,
         wg_ref, wo_ref, xo_ref, ho_ref, wgo_ref, woo_ref, acc_ref) = refs
        wgo_ref[...] = wg_ref[...].astype(BF16)
        woo_ref[...] = wo_ref[...].astype(BF16)
    else:
        h_ref, w1_ref, w2_ref, x_ref, gpost_ref, gt_ref, yp_ref, ys_ref, acc_ref = refs
    step = pl.program_id(1)
    last = n_ff_steps - 1
    assert last >= 2

    def ff_slice(rws):
        hid = jnp.square(jnp.maximum(_dot(h_ref[rws, :], w1_ref[...]), 0.0)).astype(BF16)
        return _dot(hid, w2_ref[...])

    whole = slice(None)

    @pl.when(step == 0)
    def _():
        acc_ref[...] = ff_slice(whole)

    @pl.when((step > 0) & (step < last))
    def _():
        acc_ref[...] += ff_slice(whole)

    @pl.when(step == last)
    def _():
        post = gpost_ref[...] * gt_ref[...]
        if has_next:
            pre = gpre_ref[...] * (1.0 + sc_ref[...])
            shift = sh_ref[...]
        for c in range(tm // SUB_ROWS):
            rws = slice(c * SUB_ROWS, (c + 1) * SUB_ROWS)
            x2 = x_ref[rws, :] + _unit_rms(acc_ref[rws, :] + ff_slice(rws)) * post
            if has_next:
                xo_ref[rws, :] = x2
                ho_ref[rws, :] = (_unit_rms(x2) * pre + shift).astype(ho_ref.dtype)
            else:
                acc_ref[rws, :] = x2
        if not has_next:
            is_ctx = pl.program_id(0) < rows.m_ctx // tm

            @pl.when(is_ctx)
            def _():
                yp_ref[...] = acc_ref[...]

            @pl.when(jnp.logical_not(is_ctx))
            def _():
                ys_ref[...] = acc_ref[...]


def _mlp(rows, h2, w1, w2, x, g_post_mlp, g_pre_next, mods, w_in, col_gates, w_out, layer, has_next):
    tm, tf = 512, 1024
    d = rows.d
    d_ff = w1.shape[1]
    n_ff_steps = d_ff // tf
    row_spec = pl.BlockSpec((tm, d), lambda i, f: (i, 0))
    in_specs = [
        row_spec,
        pl.BlockSpec((d, tf), lambda i, f: (0, f)),
        pl.BlockSpec((tf, d), lambda i, f: (f, 0)),
        row_spec,
        rows.gain_spec(layer),
        rows.mod_spec(layer, MOD_GATE2, tm),
    ]
    args = [h2, w1, w2, x, g_post_mlp, mods]
    if has_next:
        in_specs += [rows.gain_spec(layer + 1),
                     rows.mod_spec(layer + 1, MOD_SCALE1, tm),
                     rows.mod_spec(layer + 1, MOD_SHIFT1, tm)]
        n_steps = (rows.m // tm) * n_ff_steps
        n_slabs = 1 << (n_steps.bit_length() - 1)
        gate_width = w_in.shape[2] - col_gates
        piece = math.gcd(col_gates, gate_width)
        n_pieces = gate_width // piece
        g_rows, o_rows = d * n_pieces // n_slabs, d // n_slabs
        assert g_rows % 16 == 0 and o_rows % 16 == 0 and g_rows * n_slabs == d * n_pieces

        def slab(i, f):
            return jnp.minimum(i * n_ff_steps + f, n_slabs - 1)

        in_specs += [
            pl.BlockSpec((None, g_rows, piece),
                         lambda i, f: (layer + 1, slab(i, f) // n_pieces,
                                       col_gates // piece + slab(i, f) % n_pieces)),
            pl.BlockSpec((None, o_rows, d), lambda i, f: (layer + 1, slab(i, f), 0)),
        ]
        args += [g_pre_next, mods, mods, w_in, w_out]
        out_specs = [row_spec, row_spec,
                     pl.BlockSpec((g_rows, piece),
                                  lambda i, f: (slab(i, f) // n_pieces, slab(i, f) % n_pieces)),
                     pl.BlockSpec((o_rows, d), lambda i, f: (slab(i, f), 0))]
        out_shape = [jax.ShapeDtypeStruct((rows.m, d), F32),
                     jax.ShapeDtypeStruct((rows.m, d), BF16),
                     jax.ShapeDtypeStruct((d, gate_width), BF16),
                     jax.ShapeDtypeStruct((d, d), BF16)]
    else:
        out_specs = list(rows.split_specs(tm, d))
        out_shape = [jax.ShapeDtypeStruct((rows.m_ctx, d), F32),
                     jax.ShapeDtypeStruct((rows.m_lat, d), F32)]
    return pl.pallas_call(
        functools.partial(_mlp_kernel, n_ff_steps=n_ff_steps, has_next=has_next, rows=rows, tm=tm),
        name="mlp",
        grid=(rows.m // tm, n_ff_steps),
        in_specs=in_specs,
        out_specs=out_specs,
        out_shape=out_shape,
        scratch_shapes=[pltpu.VMEM((tm, d), F32)],
        compiler_params=_params(("arbitrary", "arbitrary"), 56),
    )(*args)


def kernel(x_prompt, x_sample, cache_k, cache_v, c, c_ctx, w_ada, b_ada, g_pre_mix, g_post_mix,
           g_pre_mlp, g_post_mlp, w_in, rpb, g_sgu, w_spatial, b_spatial, w_br_a, w_br_f, w_br_c,
           w_out, w_mlp1, w_mlp2):
    batch, seq, d = x_prompt.shape
    dec_batch, dec_seq, _ = x_sample.shape
    depth = w_in.shape[0]
    past = cache_k.shape[2]
    w_a = w_br_a.shape[1]
    w_f = w_br_f.shape[1]
    w_c = w_br_c.shape[1]
    n_heads = w_a // HEAD_DIM
    m_ctx = batch * seq
    rows = _Rows(m_ctx, dec_batch, dec_seq, d)
    assert dec_seq % GRID_W == 0 and m_ctx % dec_seq == 0 and MOD_ROWS >= 1 + dec_batch

    cvec = jnp.zeros((MOD_ROWS, d), F32).at[0].set(c_ctx).at[1:1 + dec_batch].set(c)
    mods = _ada(cvec, w_ada, b_ada)
    mods = mods[:, :1 + dec_batch].reshape(depth, 1 + dec_batch, N_MOD, d)
    mods = mods.transpose(0, 2, 1, 3)[:, :, :, None, :]

    col_f = 3 * w_a
    col_uv = col_f + w_f
    col_gates = col_uv + 2 * w_c

    w_gates_b = w_in[0, :, col_gates:].astype(BF16)
    w_out_b = w_out[0].astype(BF16)
    w_br_a_b, w_br_f_b, w_br_c_b = w_br_a.astype(BF16), w_br_f.astype(BF16), w_br_c.astype(BF16)
    w_sp_b = w_spatial.astype(BF16)
    b_sp_t = b_spatial.transpose(0, 2, 1)
    g_sgu_row = g_sgu.reshape(depth, 1, w_c)
    gains = [g.reshape(depth, 1, d) for g in (g_pre_mix, g_post_mix, g_pre_mlp, g_post_mlp)]
    g_pre_mix_r, g_post_mix_r, g_pre_mlp_r, g_post_mlp_r = gains
    cache_k_r = cache_k.reshape(dec_batch, depth, past, w_a)
    cache_v_r = cache_v.reshape(dec_batch, depth, past, w_a)
    bias = _window_bias(rpb, dec_seq // GRID_W)

    x, h = _prologue(rows, x_prompt.reshape(m_ctx, d), x_sample.reshape(rows.m_lat, d),
                     g_pre_mix_r, mods, 0)
    new_k = jnp.zeros((batch, depth, seq, n_heads, HEAD_DIM), F32)
    new_v = jnp.zeros((batch, depth, seq, n_heads, HEAD_DIM), F32)
    for l in range(depth):
        proj = _project(h, w_in, l, col_gates, 1536, F32)
        o_a_ctx, new_k, new_v = _ctx_attention(proj, new_k, new_v, l, depth, batch, seq, w_a)
        o_a_lat = _lat_attention(proj, cache_k_r, cache_v_r, bias, l, m_ctx, dec_batch, dec_seq, w_a)
        o_f_ctx = _fourier(proj, col_f, w_f, batch, seq, 0)
        o_f_lat = _fourier(proj, col_f, w_f, dec_batch, dec_seq, m_ctx // dec_seq)
        o_c = _spatial_gating(proj, col_uv, w_c, g_sgu_row, w_sp_b, b_sp_t, l)
        merged, w1_b, w2_b = _merge(rows, h, o_a_ctx, o_a_lat, o_f_ctx, o_f_lat, o_c, w_gates_b,
                                    w_br_a_b, w_br_f_b, w_br_c_b, w_mlp1, w_mlp2, l)
        x, h2 = _out_projection(rows, merged, w_out_b, x, g_post_mix_r, g_pre_mlp_r, mods, l)
        outs = _mlp(rows, h2, w1_b, w2_b, x, g_post_mlp_r, g_pre_mix_r, mods, w_in, col_gates,
                    w_out, l, l + 1 < depth)
        if l + 1 < depth:
            x, h, w_gates_b, w_out_b = outs

    y_p = outs[0].reshape(batch, seq, d)
    y_s = outs[1].reshape(dec_batch, dec_seq, d)
    cache_shape = (batch, depth, seq, n_heads, HEAD_DIM)
    return (y_p, y_s, new_k.reshape(cache_shape), new_v.reshape(cache_shape))
```

```python
import functools
import math

import numpy as np
import jax
import jax.numpy as jnp
from jax import lax
from jax.experimental import pallas as pl
from jax.experimental.pallas import tpu as pltpu

F32 = jnp.float32
BF16 = jnp.bfloat16

RMS_EPS = 1e-6
MASKED_SCORE = -1e30

HEAD_DIM = 128
GROUP_DIM = 128
GRID_W = 64
CHUNK = 128
MAX_WIN_R = 8
WIN_C = 16
N_MOD = 6
MOD_SHIFT1, MOD_SCALE1, MOD_GATE1, MOD_SHIFT2, MOD_SCALE2, MOD_GATE2 = range(N_MOD)
LANES, SUBLANES = 128, 8
MOD_ROWS = SUBLANES

MIB = 2 ** 20


def _params(semantics, vmem_mib):
    return pltpu.CompilerParams(dimension_semantics=semantics, vmem_limit_bytes=vmem_mib * MIB)


def _rms(x, g):
    return x * lax.rsqrt(jnp.mean(x * x, axis=-1, keepdims=True) + RMS_EPS) * g


def _dot(a, b):
    return jnp.dot(a, b, preferred_element_type=F32)


def _dot_nt(a, b):
    return lax.dot_general(a, b, (((1,), (1,)), ((), ())), preferred_element_type=F32)


def _ada_kernel(c_ref, w_ref, b_ref, o_ref):
    c = c_ref[...]
    s = c * jax.nn.sigmoid(c)
    o_ref[...] = _dot(s.astype(BF16), w_ref[...].astype(BF16)) + b_ref[...]


def _ada(cvec, w_ada, b_ada):
    depth, d, width = w_ada.shape
    tn = 1024
    return pl.pallas_call(
        _ada_kernel,
        name="ada",
        grid=(depth, width // tn),
        in_specs=[
            pl.BlockSpec((MOD_ROWS, d), lambda l, n: (0, 0)),
            pl.BlockSpec((None, d, tn), lambda l, n: (l, 0, n)),
            pl.BlockSpec((None, 1, tn), lambda l, n: (l, 0, n)),
        ],
        out_specs=pl.BlockSpec((None, MOD_ROWS, tn), lambda l, n: (l, 0, n)),
        out_shape=jax.ShapeDtypeStruct((depth, MOD_ROWS, width), F32),
        compiler_params=_params(("parallel", "parallel"), 40),
    )(cvec, w_ada, b_ada.reshape(depth, 1, width))


class _Rows:
    def __init__(self, m_ctx, dec_batch, dec_seq, d):
        self.m_ctx, self.dec_batch, self.dec_seq, self.d = m_ctx, dec_batch, dec_seq, d
        self.m_lat = dec_batch * dec_seq
        self.m = m_ctx + self.m_lat

    def group(self, i, tm):
        n_ctx = self.m_ctx // tm
        per_batch = self.dec_seq // tm
        return jnp.where(i < n_ctx, 0, 1 + (i - n_ctx) // per_batch)

    def mod_spec(self, layer, piece, tm, lag=0):
        return pl.BlockSpec((None, None, None, 1, self.d),
                            lambda i, *_: (layer, piece, self.group(jnp.maximum(i - lag, 0), tm), 0, 0))

    def gain_spec(self, layer):
        return pl.BlockSpec((None, 1, self.d), lambda i, *_: (layer, 0, 0))

    def split_specs(self, tm, width):
        n_ctx = self.m_ctx // tm
        ctx = pl.BlockSpec((tm, width), lambda i, *_: (jnp.minimum(i, n_ctx - 1), 0))
        lat = pl.BlockSpec((tm, width), lambda i, *_: (jnp.maximum(i - n_ctx, 0), 0))
        return ctx, lat

    def pick(self, tm, ctx_ref, lat_ref):
        return jnp.where(pl.program_id(0) < self.m_ctx // tm, ctx_ref[...], lat_ref[...])


def _prologue_kernel(xp_ref, xs_ref, g_ref, sc_ref, sh_ref, x_ref, h_ref, *, rows, tm):
    x = rows.pick(tm, xp_ref, xs_ref)
    x_ref[...] = x
    h = _rms(x, g_ref[...]) * (1.0 + sc_ref[...]) + sh_ref[...]
    h_ref[...] = h.astype(h_ref.dtype)


def _prologue(rows, x_ctx, x_lat, g_pre, mods, layer):
    tm = 512
    row_spec = pl.BlockSpec((tm, rows.d), lambda i: (i, 0))
    return pl.pallas_call(
        functools.partial(_prologue_kernel, rows=rows, tm=tm),
        name="prologue",
        grid=(rows.m // tm,),
        in_specs=[
            *rows.split_specs(tm, rows.d),
            rows.gain_spec(layer),
            rows.mod_spec(layer, MOD_SCALE1, tm),
            rows.mod_spec(layer, MOD_SHIFT1, tm),
        ],
        out_specs=[row_spec, row_spec],
        out_shape=[jax.ShapeDtypeStruct((rows.m, rows.d), F32),
                   jax.ShapeDtypeStruct((rows.m, rows.d), BF16)],
        compiler_params=_params(("arbitrary",), 40),
    )(x_ctx, x_lat, g_pre, mods, mods)


def _inproj_kernel(a_ref, w_ref, o_ref, wb_ref):
    @pl.when(pl.program_id(1) == 0)
    def _():
        wb_ref[...] = w_ref[...].astype(BF16)

    o_ref[...] = _dot(a_ref[...], wb_ref[...]).astype(o_ref.dtype)


def _project(h, w, layer, width, tn, out_dtype):
    m, k = h.shape
    tm = 1024
    assert width % tn == 0
    return pl.pallas_call(
        _inproj_kernel,
        name="in_proj",
        grid=(width // tn, m // tm),
        in_specs=[
            pl.BlockSpec((tm, k), lambda n, i: (i, 0)),
            pl.BlockSpec((None, k, tn), lambda n, i: (layer, 0, n)),
        ],
        out_specs=pl.BlockSpec((tm, tn), lambda n, i: (i, n)),
        out_shape=jax.ShapeDtypeStruct((m, width), out_dtype),
        scratch_shapes=[pltpu.VMEM((k, tn), BF16)],
        compiler_params=_params(("arbitrary", "arbitrary"), 58),
    )(h, w)


def _ctx_attn_kernel(q_ref, k_ref, v_ref, o_ref, ko_ref, vo_ref, *, n_heads, scale):
    ko_ref[...] = k_ref[...].reshape(ko_ref.shape)
    vo_ref[...] = v_ref[...].reshape(vo_ref.shape)
    for h in range(n_heads):
        cols = slice(h * HEAD_DIM, (h + 1) * HEAD_DIM)
        q = q_ref[:, cols].astype(BF16)
        k = k_ref[:, cols].astype(BF16)
        v = v_ref[:, cols].astype(BF16)
        s = _dot_nt(q, k) * scale
        p = jnp.exp(s - jnp.max(s, axis=-1, keepdims=True))
        denom = jnp.sum(p, axis=-1, keepdims=True)
        o_ref[:, cols] = (_dot(p.astype(BF16), v) / denom).astype(o_ref.dtype)


def _ctx_mixers(proj, new_k, new_v, layer, depth, batch, seq, w_a, col_f, w_f):
    n_heads = w_a // HEAD_DIM
    attend = functools.partial(_ctx_attn_kernel, n_heads=n_heads, scale=1.0 / math.sqrt(HEAD_DIM))
    fourier, tables, table_specs = _fourier_operands(seq, w_f)
    f_block = col_f // w_f
    assert f_block * w_f == col_f
    cache_spec = pl.BlockSpec((None, None, seq, n_heads, HEAD_DIM), lambda b: (b, layer, 0, 0, 0))
    cache_shape = jax.ShapeDtypeStruct((batch, depth, seq, n_heads, HEAD_DIM), F32)
    in_specs = [pl.BlockSpec((seq, w_a), lambda b, j=j: (b, j)) for j in range(3)]
    in_specs += [pl.BlockSpec((seq, w_f), lambda b: (b, f_block))] + table_specs
    in_specs += [pl.BlockSpec(memory_space=pl.ANY)] * 2
    n_in = len(in_specs)

    def body(q_ref, k_ref, v_ref, f_ref, wc_ref, cn_ref, sn_ref, k_all_ref, v_all_ref,
             o_ref, of_ref, ko_ref, vo_ref):
        del k_all_ref, v_all_ref
        attend(q_ref, k_ref, v_ref, o_ref, ko_ref, vo_ref)
        fourier(f_ref, wc_ref, cn_ref, sn_ref, of_ref)

    return pl.pallas_call(
        body,
        name="ctx_mixers",
        grid=(batch,),
        in_specs=in_specs,
        out_specs=[pl.BlockSpec((seq, w_a), lambda b: (b, 0)),
                   pl.BlockSpec((seq, w_f), lambda b: (b, 0)), cache_spec, cache_spec],
        out_shape=[jax.ShapeDtypeStruct((batch * seq, w_a), BF16),
                   jax.ShapeDtypeStruct((batch * seq, w_f), BF16), cache_shape, cache_shape],
        input_output_aliases={n_in - 2: 2, n_in - 1: 3},
        compiler_params=_params(("arbitrary",), 32),
    )(proj, proj, proj, proj, *tables, new_k, new_v)


def _window_start(r, rows):
    win_r = min(MAX_WIN_R, rows)
    return min(max(r - win_r // 2, 0), rows - win_r)


QUERY_GROUP_ROWS = 4


def _query_groups(rows):
    win_r = min(MAX_WIN_R, rows)
    span = min(rows, win_r + QUERY_GROUP_ROWS)
    assert rows % QUERY_GROUP_ROWS == 0
    groups = []
    for r0 in range(0, rows, QUERY_GROUP_ROWS):
        k0 = min(_window_start(r0, rows), rows - span)
        for r in range(r0, r0 + QUERY_GROUP_ROWS):
            assert k0 <= _window_start(r, rows) and _window_start(r, rows) + win_r <= k0 + span
        groups.append((r0, k0))
    return groups, span


def _bias_kernel(rpb_ref, o_ref, *, rows):
    win_r = min(MAX_WIN_R, rows)
    lanes = rpb_ref.shape[1]
    q = lax.broadcasted_iota(jnp.int32, (GRID_W, lanes), 0)
    kc = lax.broadcasted_iota(jnp.int32, (GRID_W, lanes), 1)
    c_start = jnp.clip(q - WIN_C // 2, 0, GRID_W - WIN_C)
    valid = (kc >= c_start) & (kc < c_start + WIN_C)
    tables = []
    for dr in range(2 * MAX_WIN_R - 1):
        row = jnp.broadcast_to(rpb_ref[dr:dr + 1, :], (GRID_W, lanes))
        t = pltpu.roll(row, lanes - (WIN_C - 1), 1, stride=1, stride_axis=0)
        tables.append(jnp.where(valid, t, MASKED_SCORE)[:, :GRID_W])
    masked = jnp.full((GRID_W, GRID_W), MASKED_SCORE, F32)
    groups, span = _query_groups(rows)
    for g, (r0, k0) in enumerate(groups):
        for j in range(QUERY_GROUP_ROWS):
            start = _window_start(r0 + j, rows)
            for i in range(span):
                in_window = start <= k0 + i < start + win_r
                dr = k0 + i - (r0 + j) + (MAX_WIN_R - 1)
                o_ref[g, j * GRID_W:(j + 1) * GRID_W, i * GRID_W:(i + 1) * GRID_W] = (
                    tables[dr] if in_window else masked)


def _window_bias(rpb, rows):
    depth, n_heads, n_dr, n_dc = rpb.shape
    groups, span = _query_groups(rows)
    block = (len(groups), QUERY_GROUP_ROWS * GRID_W, span * GRID_W)
    dr_pad = -(-n_dr // SUBLANES) * SUBLANES
    rpb_rows = jnp.pad(rpb.reshape(depth * n_heads, n_dr, n_dc),
                       ((0, 0), (0, dr_pad - n_dr), (0, LANES - n_dc)))
    return pl.pallas_call(
        functools.partial(_bias_kernel, rows=rows),
        name="window_bias",
        grid=(depth * n_heads,),
        in_specs=[pl.BlockSpec((None, dr_pad, LANES), lambda i: (i, 0, 0))],
        out_specs=pl.BlockSpec((None,) + block, lambda i: (i, 0, 0, 0)),
        out_shape=jax.ShapeDtypeStruct((depth * n_heads,) + block, F32),
        compiler_params=_params(("parallel",), 32),
    )(rpb_rows)


def _lat_attn_kernel(q_ref, k_ref, v_ref, ck_ref, cv_ref, bias_ref, o_ref, *, rows, scale):
    groups, span = _query_groups(rows)
    ck = ck_ref[...].astype(BF16)
    cv = cv_ref[...].astype(BF16)
    for g, (r0, k0) in enumerate(groups):
        q_rows = slice(r0 * GRID_W, (r0 + QUERY_GROUP_ROWS) * GRID_W)
        k_rows = slice(k0 * GRID_W, (k0 + span) * GRID_W)
        q = q_ref[q_rows, :].astype(BF16)
        kw = k_ref[k_rows, :].astype(BF16)
        vw = v_ref[k_rows, :].astype(BF16)
        s_lat = _dot_nt(q, kw) * scale + bias_ref[g]
        s_ctx = _dot_nt(q, ck) * scale
        top = jnp.maximum(jnp.max(s_lat, axis=-1, keepdims=True),
                          jnp.max(s_ctx, axis=-1, keepdims=True))
        p_lat = jnp.exp(s_lat - top)
        p_ctx = jnp.exp(s_ctx - top)
        denom = jnp.sum(p_lat, axis=-1, keepdims=True) + jnp.sum(p_ctx, axis=-1, keepdims=True)
        o = _dot(p_lat.astype(BF16), vw) + _dot(p_ctx.astype(BF16), cv)
        o_ref[q_rows, :] = (o / denom).astype(o_ref.dtype)


def _lat_attention(proj, cache_k, cache_v, bias, layer, m_ctx, dec_batch, dec_seq, w_a):
    n_heads = w_a // HEAD_DIM
    rows = dec_seq // GRID_W
    past = cache_k.shape[2]
    first = m_ctx // dec_seq
    assert first * dec_seq == m_ctx
    kern = functools.partial(_lat_attn_kernel, rows=rows, scale=1.0 / math.sqrt(HEAD_DIM))
    qkv_specs = [pl.BlockSpec((dec_seq, HEAD_DIM), lambda h, b, j=j: (first + b, j * n_heads + h))
                 for j in range(3)]
    cache_spec = pl.BlockSpec((None, None, past, HEAD_DIM), lambda h, b: (b, layer, 0, h))
    return pl.pallas_call(
        kern,
        name="lat_attn",
        grid=(n_heads, dec_batch),
        in_specs=qkv_specs + [
            cache_spec, cache_spec,
            pl.BlockSpec((None,) + bias.shape[1:], lambda h, b: (layer * n_heads + h, 0, 0, 0)),
        ],
        out_specs=pl.BlockSpec((dec_seq, HEAD_DIM), lambda h, b: (b, h)),
        out_shape=jax.ShapeDtypeStruct((dec_batch * dec_seq, w_a), BF16),
        compiler_params=_params(("parallel", "parallel"), 32),
    )(proj, proj, proj, cache_k, cache_v, bias)


def _dft_tables(n):
    idx = np.arange(n, dtype=np.int64)
    ang = 2.0 * np.pi * ((idx[:, None] * idx[None, :]) % n) / n
    return np.cos(ang), np.sin(ang)


def _fourier_kernel(f_ref, wc_ref, cn_ref, sn_ref, o_ref, *, n_groups, scale):
    wc = wc_ref[...].astype(BF16)
    pc, ps = [], []
    for g in range(n_groups):
        p = _dot(f_ref[:, g * GROUP_DIM:(g + 1) * GROUP_DIM].astype(BF16), wc)
        pc.append(p[:, :GROUP_DIM])
        ps.append(p[:, GROUP_DIM:])
    y = (_dot(cn_ref[...].astype(BF16), jnp.concatenate(pc, axis=1).astype(BF16))
         - _dot(sn_ref[...].astype(BF16), jnp.concatenate(ps, axis=1).astype(BF16)))
    o_ref[...] = (y * scale).astype(o_ref.dtype)


def _fourier_operands(n_pos, w_f):
    cc, sc = _dft_tables(GROUP_DIM)
    cn, sn = _dft_tables(n_pos)
    tables = tuple(jnp.asarray(t, F32) for t in (np.concatenate([cc, sc], axis=1), cn, sn))
    specs = [pl.BlockSpec(t.shape, lambda b: (0, 0)) for t in tables]
    kern = functools.partial(_fourier_kernel, n_groups=w_f // GROUP_DIM,
                             scale=1.0 / math.sqrt(n_pos * GROUP_DIM))
    return kern, tables, specs


def _fourier(proj, col_start, w_f, n_batch, n_pos, first_block):
    col_block = col_start // w_f
    assert col_block * w_f == col_start
    kern, tables, table_specs = _fourier_operands(n_pos, w_f)
    return pl.pallas_call(
        kern,
        name="fourier",
        grid=(n_batch,),
        in_specs=[pl.BlockSpec((n_pos, w_f), lambda b: (first_block + b, col_block))] + table_specs,
        out_specs=pl.BlockSpec((n_pos, w_f), lambda b: (b, 0)),
        out_shape=jax.ShapeDtypeStruct((n_batch * n_pos, w_f), BF16),
        compiler_params=_params(("parallel",), 48),
    )(proj, *tables)


def _gelu(x):
    return 0.5 * x * (1.0 + lax.erf(x * math.sqrt(0.5)))


def _sgu_kernel(u_ref, v_ref, g_ref, w_ref, b_ref, o_ref, *, n_groups, n_chunks):
    for g in range(n_groups):
        cols = slice(g * GROUP_DIM, (g + 1) * GROUP_DIM)
        v = _rms(_gelu(v_ref[:, cols]), g_ref[:, cols]).astype(BF16)
        v_wide = jnp.concatenate([v[c * CHUNK:(c + 1) * CHUNK, :] for c in range(n_chunks)], axis=1)
        s = _dot(w_ref[g], v_wide) + b_ref[:, g:g + 1]
        for c in range(n_chunks):
            rws = slice(c * CHUNK, (c + 1) * CHUNK)
            o_ref[rws, cols] = (_gelu(u_ref[rws, cols])
                                * s[:, c * GROUP_DIM:(c + 1) * GROUP_DIM]).astype(o_ref.dtype)


def _spatial_gating(proj, col_start, w_c, g_sgu, w_sp, b_sp_t, layer):
    m = proj.shape[0]
    n_groups = w_c // GROUP_DIM
    n_chunks = 4
    tm = n_chunks * CHUNK
    col_block = col_start // w_c
    assert col_block * w_c == col_start
    kern = functools.partial(_sgu_kernel, n_groups=n_groups, n_chunks=n_chunks)
    return pl.pallas_call(
        kern,
        name="spatial_gate",
        grid=(m // tm,),
        in_specs=[
            pl.BlockSpec((tm, w_c), lambda i: (i, col_block)),
            pl.BlockSpec((tm, w_c), lambda i: (i, col_block + 1)),
            pl.BlockSpec((None, 1, w_c), lambda i: (layer, 0, 0)),
            pl.BlockSpec((None, n_groups, CHUNK, CHUNK), lambda i: (layer, 0, 0, 0)),
            pl.BlockSpec((None, CHUNK, n_groups), lambda i: (layer, 0, 0)),
        ],
        out_specs=pl.BlockSpec((tm, w_c), lambda i: (i, 0)),
        out_shape=jax.ShapeDtypeStruct((m, w_c), BF16),
        compiler_params=_params(("parallel",), 32),
    )(proj, proj, g_sgu, w_sp, b_sp_t)


def _merge_kernel(h_ref, oa_ctx_ref, oa_lat_ref, of_ctx_ref, of_lat_ref, oc_ref,
                  wga_ref, wgf_ref, wgc_ref, wa_ref, wf_ref, wc_ref, w1_ref, w2_ref,
                  o_ref, w1o_ref, w2o_ref, *, rows, tm):
    w1o_ref[...] = w1_ref[...].astype(BF16)
    w2o_ref[...] = w2_ref[...].astype(BF16)
    h = h_ref[...]
    o_a = rows.pick(tm, oa_ctx_ref, oa_lat_ref)
    o_f = rows.pick(tm, of_ctx_ref, of_lat_ref)
    acc = jax.nn.sigmoid(_dot(h, wga_ref[...])) * _dot(o_a, wa_ref[...])
    acc += jax.nn.sigmoid(_dot(h, wgf_ref[...])) * _dot(o_f, wf_ref[...])
    acc += jax.nn.sigmoid(_dot(h, wgc_ref[...])) * _dot(oc_ref[...], wc_ref[...])
    o_ref[...] = acc.astype(o_ref.dtype)


def _merge(rows, h, o_a_ctx, o_a_lat, o_f_ctx, o_f_lat, o_c, w_gates, w_br_a, w_br_f, w_br_c,
           w_mlp1, w_mlp2, layer):
    d = rows.d
    d_ff = w_mlp1.shape[2]
    tm, tn = 1024, 256
    nb = d // tn
    n_steps = (rows.m // tm) * (nb - 1)
    n_slabs = 1 << (n_steps.bit_length() - 1)
    slab1, slab2 = d // n_slabs, d_ff // n_slabs
    assert slab1 * n_slabs == d and slab2 * n_slabs == d_ff and slab1 % 16 == 0

    def slab(i, n):
        return jnp.minimum(i * (nb - 1) + jnp.minimum(n, nb - 2), n_slabs - 1)

    full = lambda width: pl.BlockSpec((tm, width), lambda i, n: (i, 0))
    gate_w = lambda j: pl.BlockSpec((d, tn), lambda i, n: (0, j * nb + n))
    br_w = lambda w: pl.BlockSpec((None, w.shape[1], tn), lambda i, n: (layer, 0, n))
    return pl.pallas_call(
        functools.partial(_merge_kernel, rows=rows, tm=tm),
        name="gate_merge",
        grid=(rows.m // tm, nb),
        in_specs=[full(d),
                  *rows.split_specs(tm, o_a_ctx.shape[1]),
                  *rows.split_specs(tm, o_f_ctx.shape[1]),
                  full(o_c.shape[1]),
                  gate_w(0), gate_w(1), gate_w(2),
                  br_w(w_br_a), br_w(w_br_f), br_w(w_br_c),
                  pl.BlockSpec((None, slab1, d_ff), lambda i, n: (layer, slab(i, n), 0)),
                  pl.BlockSpec((None, slab2, d), lambda i, n: (layer, slab(i, n), 0))],
        out_specs=[pl.BlockSpec((tm, tn), lambda i, n: (i, n)),
                   pl.BlockSpec((slab1, d_ff), lambda i, n: (slab(i, n), 0)),
                   pl.BlockSpec((slab2, d), lambda i, n: (slab(i, n), 0))],
        out_shape=[jax.ShapeDtypeStruct((rows.m, d), BF16),
                   jax.ShapeDtypeStruct((d, d_ff), BF16),
                   jax.ShapeDtypeStruct((d_ff, d), BF16)],
        compiler_params=_params(("arbitrary", "arbitrary"), 56),
    )(h, o_a_ctx, o_a_lat, o_f_ctx, o_f_lat, o_c, w_gates, w_gates, w_gates,
      w_br_a, w_br_f, w_br_c, w_mlp1, w_mlp2)


SUB_ROWS = 256


def _unit_rms(y):
    return y * lax.rsqrt(jnp.mean(y * y, axis=-1, keepdims=True) + RMS_EPS)


def _outproj_kernel(mg_ref, w_ref, x_ref, gpost_ref, gt_ref, gpre_ref, sc_ref, sh_ref,
                    xo_ref, ho_ref, y_even_ref, y_odd_ref, *, n_tiles):
    s = pl.program_id(0)
    y_refs = (y_even_ref, y_odd_ref)

    def multiply(parity):
        y_refs[parity][...] = _dot(mg_ref[...], w_ref[...])

    def finish(parity):
        post = gpost_ref[...] * gt_ref[...]
        pre = gpre_ref[...] * (1.0 + sc_ref[...])
        x1 = x_ref[...] + _unit_rms(y_refs[parity][...]) * post
        xo_ref[...] = x1
        ho_ref[...] = (_unit_rms(x1) * pre + sh_ref[...]).astype(ho_ref.dtype)

    @pl.when(s == 0)
    def _():
        multiply(0)

    for parity in (0, 1):
        @pl.when((s > 0) & (s < n_tiles) & (s % 2 == parity))
        def _():
            multiply(parity)
            finish(1 - parity)

    @pl.when(s == n_tiles)
    def _():
        finish((n_tiles - 1) % 2)


def _out_projection(rows, merged, w_out, x, g_post_mix, g_pre_mlp, mods, layer):
    tm = 512
    d = rows.d
    n_tiles = rows.m // tm
    ahead_spec = pl.BlockSpec((tm, d), lambda s: (jnp.minimum(s, n_tiles - 1), 0))
    lag_spec = pl.BlockSpec((tm, d), lambda s: (jnp.maximum(s - 1, 0), 0))
    return pl.pallas_call(
        functools.partial(_outproj_kernel, n_tiles=n_tiles),
        name="out_proj",
        grid=(n_tiles + 1,),
        in_specs=[
            ahead_spec,
            pl.BlockSpec((d, d), lambda s: (0, 0), pipeline_mode=pl.Buffered(1)),
            lag_spec,
            rows.gain_spec(layer),
            rows.mod_spec(layer, MOD_GATE1, tm, lag=1),
            rows.gain_spec(layer),
            rows.mod_spec(layer, MOD_SCALE2, tm, lag=1),
            rows.mod_spec(layer, MOD_SHIFT2, tm, lag=1),
        ],
        out_specs=[lag_spec, lag_spec],
        out_shape=[jax.ShapeDtypeStruct((rows.m, d), F32), jax.ShapeDtypeStruct((rows.m, d), BF16)],
        scratch_shapes=[pltpu.VMEM((tm, d), F32), pltpu.VMEM((tm, d), F32)],
        compiler_params=_params(("arbitrary",), 52),
    )(merged, w_out, x, g_post_mix, mods, g_pre_mlp, mods, mods)


def _mlp_kernel(*refs, n_ff_steps, has_next, rows, tm):
    if has_next:
        (h_ref, w1_ref, w2_ref, x_ref, gpost_ref, gt_ref, gpre_ref, sc_ref, sh_ref,
         wg_ref, wo_ref, xo_ref, ho_ref, wgo_ref, woo_ref, acc_ref) = refs
        wgo_ref[...] = wg_ref[...].astype(BF16)
        woo_ref[...] = wo_ref[...].astype(BF16)
    else:
        h_ref, w1_ref, w2_ref, x_ref, gpost_ref, gt_ref, yp_ref, ys_ref, acc_ref = refs
    step = pl.program_id(1)
    last = n_ff_steps - 1
    assert last >= 2

    def ff_slice(rws):
        hid = jnp.square(jnp.maximum(_dot(h_ref[rws, :], w1_ref[...]), 0.0)).astype(BF16)
        return _dot(hid, w2_ref[...])

    whole = slice(None)

    @pl.when(step == 0)
    def _():
        acc_ref[...] = ff_slice(whole)

    @pl.when((step > 0) & (step < last))
    def _():
        acc_ref[...] += ff_slice(whole)

    @pl.when(step == last)
    def _():
        post = gpost_ref[...] * gt_ref[...]
        if has_next:
            pre = gpre_ref[...] * (1.0 + sc_ref[...])
            shift = sh_ref[...]
        for c in range(tm // SUB_ROWS):
            rws = slice(c * SUB_ROWS, (c + 1) * SUB_ROWS)
            x2 = x_ref[rws, :] + _unit_rms(acc_ref[rws, :] + ff_slice(rws)) * post
            if has_next:
                xo_ref[rws, :] = x2
                ho_ref[rws, :] = (_unit_rms(x2) * pre + shift).astype(ho_ref.dtype)
            else:
                acc_ref[rws, :] = x2
        if not has_next:
            is_ctx = pl.program_id(0) < rows.m_ctx // tm

            @pl.when(is_ctx)
            def _():
                yp_ref[...] = acc_ref[...]

            @pl.when(jnp.logical_not(is_ctx))
            def _():
                ys_ref[...] = acc_ref[...]


def _mlp(rows, h2, w1, w2, x, g_post_mlp, g_pre_next, mods, w_in, col_gates, w_out, layer, has_next):
    tm, tf = 512, 1024
    d = rows.d
    d_ff = w1.shape[1]
    n_ff_steps = d_ff // tf
    row_spec = pl.BlockSpec((tm, d), lambda i, f: (i, 0))
    in_specs = [
        row_spec,
        pl.BlockSpec((d, tf), lambda i, f: (0, f)),
        pl.BlockSpec((tf, d), lambda i, f: (f, 0)),
        row_spec,
        rows.gain_spec(layer),
        rows.mod_spec(layer, MOD_GATE2, tm),
    ]
    args = [h2, w1, w2, x, g_post_mlp, mods]
    if has_next:
        in_specs += [rows.gain_spec(layer + 1),
                     rows.mod_spec(layer + 1, MOD_SCALE1, tm),
                     rows.mod_spec(layer + 1, MOD_SHIFT1, tm)]
        n_steps = (rows.m // tm) * (n_ff_steps - 1)
        n_slabs = 1 << (n_steps.bit_length() - 1)
        gate_width = w_in.shape[2] - col_gates
        piece = math.gcd(col_gates, gate_width)
        n_pieces = gate_width // piece
        g_rows, o_rows = d * n_pieces // n_slabs, d // n_slabs
        assert g_rows % 16 == 0 and o_rows % 16 == 0 and g_rows * n_slabs == d * n_pieces

        def slab(i, f):
            return jnp.minimum(i * (n_ff_steps - 1) + jnp.minimum(f, n_ff_steps - 2), n_slabs - 1)

        in_specs += [
            pl.BlockSpec((None, g_rows, piece),
                         lambda i, f: (layer + 1, slab(i, f) // n_pieces,
                                       col_gates // piece + slab(i, f) % n_pieces)),
            pl.BlockSpec((None, o_rows, d), lambda i, f: (layer + 1, slab(i, f), 0)),
        ]
        args += [g_pre_next, mods, mods, w_in, w_out]
        out_specs = [row_spec, row_spec,
                     pl.BlockSpec((g_rows, piece),
                                  lambda i, f: (slab(i, f) // n_pieces, slab(i, f) % n_pieces)),
                     pl.BlockSpec((o_rows, d), lambda i, f: (slab(i, f), 0))]
        out_shape = [jax.ShapeDtypeStruct((rows.m, d), F32),
                     jax.ShapeDtypeStruct((rows.m, d), BF16),
                     jax.ShapeDtypeStruct((d, gate_width), BF16),
                     jax.ShapeDtypeStruct((d, d), BF16)]
    else:
        out_specs = list(rows.split_specs(tm, d))
        out_shape = [jax.ShapeDtypeStruct((rows.m_ctx, d), F32),
                     jax.ShapeDtypeStruct((rows.m_lat, d), F32)]
    return pl.pallas_call(
        functools.partial(_mlp_kernel, n_ff_steps=n_ff_steps, has_next=has_next, rows=rows, tm=tm),
        name="mlp",
        grid=(rows.m // tm, n_ff_steps),
        in_specs=in_specs,
        out_specs=out_specs,
        out_shape=out_shape,
        scratch_shapes=[pltpu.VMEM((tm, d), F32)],
        compiler_params=_params(("arbitrary", "arbitrary"), 56),
    )(*args)


def kernel(x_prompt, x_sample, cache_k, cache_v, c, c_ctx, w_ada, b_ada, g_pre_mix, g_post_mix,
           g_pre_mlp, g_post_mlp, w_in, rpb, g_sgu, w_spatial, b_spatial, w_br_a, w_br_f, w_br_c,
           w_out, w_mlp1, w_mlp2):
    batch, seq, d = x_prompt.shape
    dec_batch, dec_seq, _ = x_sample.shape
    depth = w_in.shape[0]
    past = cache_k.shape[2]
    w_a = w_br_a.shape[1]
    w_f = w_br_f.shape[1]
    w_c = w_br_c.shape[1]
    n_heads = w_a // HEAD_DIM
    m_ctx = batch * seq
    rows = _Rows(m_ctx, dec_batch, dec_seq, d)
    assert dec_seq % GRID_W == 0 and m_ctx % dec_seq == 0 and MOD_ROWS >= 1 + dec_batch

    cvec = jnp.zeros((MOD_ROWS, d), F32).at[0].set(c_ctx).at[1:1 + dec_batch].set(c)
    mods = _ada(cvec, w_ada, b_ada)
    mods = mods[:, :1 + dec_batch].reshape(depth, 1 + dec_batch, N_MOD, d)
    mods = mods.transpose(0, 2, 1, 3)[:, :, :, None, :]

    col_f = 3 * w_a
    col_uv = col_f + w_f
    col_gates = col_uv + 2 * w_c

    w_gates_b = w_in[0, :, col_gates:].astype(BF16)
    w_out_b = w_out[0].astype(BF16)
    w_br_a_b, w_br_f_b, w_br_c_b = w_br_a.astype(BF16), w_br_f.astype(BF16), w_br_c.astype(BF16)
    w_sp_b = w_spatial.astype(BF16)
    b_sp_t = b_spatial.transpose(0, 2, 1)
    g_sgu_row = g_sgu.reshape(depth, 1, w_c)
    gains = [g.reshape(depth, 1, d) for g in (g_pre_mix, g_post_mix, g_pre_mlp, g_post_mlp)]
    g_pre_mix_r, g_post_mix_r, g_pre_mlp_r, g_post_mlp_r = gains
    cache_k_r = cache_k.reshape(dec_batch, depth, past, w_a)
    cache_v_r = cache_v.reshape(dec_batch, depth, past, w_a)
    bias = _window_bias(rpb, dec_seq // GRID_W)

    x, h = _prologue(rows, x_prompt.reshape(m_ctx, d), x_sample.reshape(rows.m_lat, d),
                     g_pre_mix_r, mods, 0)
    new_k = jnp.zeros((batch, depth, seq, n_heads, HEAD_DIM), F32)
    new_v = jnp.zeros((batch, depth, seq, n_heads, HEAD_DIM), F32)
    for l in range(depth):
        proj = _project(h, w_in, l, col_gates, 1536, F32)
        o_a_ctx, o_f_ctx, new_k, new_v = _ctx_mixers(proj, new_k, new_v, l, depth, batch, seq, w_a,
                                                     col_f, w_f)
        o_a_lat = _lat_attention(proj, cache_k_r, cache_v_r, bias, l, m_ctx, dec_batch, dec_seq, w_a)
        o_f_lat = _fourier(proj, col_f, w_f, dec_batch, dec_seq, m_ctx // dec_seq)
        o_c = _spatial_gating(proj, col_uv, w_c, g_sgu_row, w_sp_b, b_sp_t, l)
        merged, w1_b, w2_b = _merge(rows, h, o_a_ctx, o_a_lat, o_f_ctx, o_f_lat, o_c, w_gates_b,
                                    w_br_a_b, w_br_f_b, w_br_c_b, w_mlp1, w_mlp2, l)
        x, h2 = _out_projection(rows, merged, w_out_b, x, g_post_mix_r, g_pre_mlp_r, mods, l)
        outs = _mlp(rows, h2, w1_b, w2_b, x, g_post_mlp_r, g_pre_mix_r, mods, w_in, col_gates,
                    w_out, l, l + 1 < depth)
        if l + 1 < depth:
            x, h, w_gates_b, w_out_b = outs

    y_p = outs[0].reshape(batch, seq, d)
    y_s = outs[1].reshape(dec_batch, dec_seq, d)
    cache_shape = (batch, depth, seq, n_heads, HEAD_DIM)
    return (y_p, y_s, new_k.reshape(cache_shape), new_v.reshape(cache_shape))
```

```python
import functools
import math

import numpy as np
import jax
import jax.numpy as jnp
from jax import lax
from jax.experimental import pallas as pl
from jax.experimental.pallas import tpu as pltpu

F32 = jnp.float32
BF16 = jnp.bfloat16

RMS_EPS = 1e-6
MASKED_SCORE = -1e30

HEAD_DIM = 128
GROUP_DIM = 128
GRID_W = 64
CHUNK = 128
MAX_WIN_R = 8
WIN_C = 16
N_MOD = 6
MOD_SHIFT1, MOD_SCALE1, MOD_GATE1, MOD_SHIFT2, MOD_SCALE2, MOD_GATE2 = range(N_MOD)
LANES, SUBLANES = 128, 8
MOD_ROWS = SUBLANES

MIB = 2 ** 20


def _params(semantics, vmem_mib):
    return pltpu.CompilerParams(dimension_semantics=semantics, vmem_limit_bytes=vmem_mib * MIB)


def _rms(x, g):
    return x * lax.rsqrt(jnp.mean(x * x, axis=-1, keepdims=True) + RMS_EPS) * g


def _dot(a, b):
    return jnp.dot(a, b, preferred_element_type=F32)


def _dot_nt(a, b):
    return lax.dot_general(a, b, (((1,), (1,)), ((), ())), preferred_element_type=F32)


def _ada_kernel(c_ref, w_ref, b_ref, o_ref):
    c = c_ref[...]
    s = c * jax.nn.sigmoid(c)
    o_ref[...] = _dot(s.astype(BF16), w_ref[...].astype(BF16)) + b_ref[...]


def _ada(cvec, w_ada, b_ada):
    depth, d, width = w_ada.shape
    tn = 1024
    return pl.pallas_call(
        _ada_kernel,
        name="ada",
        grid=(depth, width // tn),
        in_specs=[
            pl.BlockSpec((MOD_ROWS, d), lambda l, n: (0, 0)),
            pl.BlockSpec((None, d, tn), lambda l, n: (l, 0, n)),
            pl.BlockSpec((None, 1, tn), lambda l, n: (l, 0, n)),
        ],
        out_specs=pl.BlockSpec((None, MOD_ROWS, tn), lambda l, n: (l, 0, n)),
        out_shape=jax.ShapeDtypeStruct((depth, MOD_ROWS, width), F32),
        compiler_params=_params(("parallel", "parallel"), 40),
    )(cvec, w_ada, b_ada.reshape(depth, 1, width))


class _Rows:
    def __init__(self, m_ctx, dec_batch, dec_seq, d):
        self.m_ctx, self.dec_batch, self.dec_seq, self.d = m_ctx, dec_batch, dec_seq, d
        self.m_lat = dec_batch * dec_seq
        self.m = m_ctx + self.m_lat

    def group(self, i, tm):
        n_ctx = self.m_ctx // tm
        per_batch = self.dec_seq // tm
        return jnp.where(i < n_ctx, 0, 1 + (i - n_ctx) // per_batch)

    def mod_spec(self, layer, piece, tm, lag=0):
        return pl.BlockSpec((None, None, None, 1, self.d),
                            lambda i, *_: (layer, piece, self.group(jnp.maximum(i - lag, 0), tm), 0, 0))

    def gain_spec(self, layer):
        return pl.BlockSpec((None, 1, self.d), lambda i, *_: (layer, 0, 0))

    def split_specs(self, tm, width):
        n_ctx = self.m_ctx // tm
        ctx = pl.BlockSpec((tm, width), lambda i, *_: (jnp.minimum(i, n_ctx - 1), 0))
        lat = pl.BlockSpec((tm, width), lambda i, *_: (jnp.maximum(i - n_ctx, 0), 0))
        return ctx, lat

    def pick(self, tm, ctx_ref, lat_ref):
        return jnp.where(pl.program_id(0) < self.m_ctx // tm, ctx_ref[...], lat_ref[...])


def _prologue_kernel(xp_ref, xs_ref, g_ref, sc_ref, sh_ref, x_ref, h_ref, *, rows, tm):
    x = rows.pick(tm, xp_ref, xs_ref)
    x_ref[...] = x
    h = _rms(x, g_ref[...]) * (1.0 + sc_ref[...]) + sh_ref[...]
    h_ref[...] = h.astype(h_ref.dtype)


def _prologue(rows, x_ctx, x_lat, g_pre, mods, layer):
    tm = 512
    row_spec = pl.BlockSpec((tm, rows.d), lambda i: (i, 0))
    return pl.pallas_call(
        functools.partial(_prologue_kernel, rows=rows, tm=tm),
        name="prologue",
        grid=(rows.m // tm,),
        in_specs=[
            *rows.split_specs(tm, rows.d),
            rows.gain_spec(layer),
            rows.mod_spec(layer, MOD_SCALE1, tm),
            rows.mod_spec(layer, MOD_SHIFT1, tm),
        ],
        out_specs=[row_spec, row_spec],
        out_shape=[jax.ShapeDtypeStruct((rows.m, rows.d), F32),
                   jax.ShapeDtypeStruct((rows.m, rows.d), BF16)],
        compiler_params=_params(("arbitrary",), 40),
    )(x_ctx, x_lat, g_pre, mods, mods)


def _inproj_kernel(a_ref, w_ref, o_ref, wb_ref):
    @pl.when(pl.program_id(1) == 0)
    def _():
        wb_ref[...] = w_ref[...].astype(BF16)

    o_ref[...] = _dot(a_ref[...], wb_ref[...]).astype(o_ref.dtype)


def _project(h, w, layer, width, tn, out_dtype):
    m, k = h.shape
    tm = 1024
    assert width % tn == 0
    return pl.pallas_call(
        _inproj_kernel,
        name="in_proj",
        grid=(width // tn, m // tm),
        in_specs=[
            pl.BlockSpec((tm, k), lambda n, i: (i, 0)),
            pl.BlockSpec((None, k, tn), lambda n, i: (layer, 0, n)),
        ],
        out_specs=pl.BlockSpec((tm, tn), lambda n, i: (i, n)),
        out_shape=jax.ShapeDtypeStruct((m, width), out_dtype),
        scratch_shapes=[pltpu.VMEM((k, tn), BF16)],
        compiler_params=_params(("arbitrary", "arbitrary"), 58),
    )(h, w)


def _ctx_attn_kernel(q_ref, k_ref, v_ref, o_ref, ko_ref, vo_ref, *, n_heads, scale):
    ko_ref[...] = k_ref[...].reshape(ko_ref.shape)
    vo_ref[...] = v_ref[...].reshape(vo_ref.shape)
    for h in range(n_heads):
        cols = slice(h * HEAD_DIM, (h + 1) * HEAD_DIM)
        q = q_ref[:, cols].astype(BF16)
        k = k_ref[:, cols].astype(BF16)
        v = v_ref[:, cols].astype(BF16)
        s = _dot_nt(q, k) * scale
        p = jnp.exp(s - jnp.max(s, axis=-1, keepdims=True))
        denom = jnp.sum(p, axis=-1, keepdims=True)
        o_ref[:, cols] = (_dot(p.astype(BF16), v) / denom).astype(o_ref.dtype)


def _ctx_mixers(proj, new_k, new_v, layer, depth, batch, seq, w_a, col_f, w_f):
    n_heads = w_a // HEAD_DIM
    attend = functools.partial(_ctx_attn_kernel, n_heads=n_heads, scale=1.0 / math.sqrt(HEAD_DIM))
    fourier, tables, table_specs = _fourier_operands(seq, w_f)
    f_block = col_f // w_f
    assert f_block * w_f == col_f
    cache_spec = pl.BlockSpec((None, None, seq, n_heads, HEAD_DIM), lambda b: (b, layer, 0, 0, 0))
    cache_shape = jax.ShapeDtypeStruct((batch, depth, seq, n_heads, HEAD_DIM), F32)
    in_specs = [pl.BlockSpec((seq, w_a), lambda b, j=j: (b, j)) for j in range(3)]
    in_specs += [pl.BlockSpec((seq, w_f), lambda b: (b, f_block))] + table_specs
    in_specs += [pl.BlockSpec(memory_space=pl.ANY)] * 2
    n_in = len(in_specs)

    def body(q_ref, k_ref, v_ref, f_ref, wc_ref, cn_ref, sn_ref, k_all_ref, v_all_ref,
             o_ref, of_ref, ko_ref, vo_ref):
        del k_all_ref, v_all_ref
        attend(q_ref, k_ref, v_ref, o_ref, ko_ref, vo_ref)
        fourier(f_ref, wc_ref, cn_ref, sn_ref, of_ref)

    return pl.pallas_call(
        body,
        name="ctx_mixers",
        grid=(batch,),
        in_specs=in_specs,
        out_specs=[pl.BlockSpec((seq, w_a), lambda b: (b, 0)),
                   pl.BlockSpec((seq, w_f), lambda b: (b, 0)), cache_spec, cache_spec],
        out_shape=[jax.ShapeDtypeStruct((batch * seq, w_a), BF16),
                   jax.ShapeDtypeStruct((batch * seq, w_f), BF16), cache_shape, cache_shape],
        input_output_aliases={n_in - 2: 2, n_in - 1: 3},
        compiler_params=_params(("arbitrary",), 32),
    )(proj, proj, proj, proj, *tables, new_k, new_v)


def _window_start(r, rows):
    win_r = min(MAX_WIN_R, rows)
    return min(max(r - win_r // 2, 0), rows - win_r)


QUERY_GROUP_ROWS = 4


def _query_groups(rows):
    win_r = min(MAX_WIN_R, rows)
    span = min(rows, win_r + QUERY_GROUP_ROWS)
    assert rows % QUERY_GROUP_ROWS == 0
    groups = []
    for r0 in range(0, rows, QUERY_GROUP_ROWS):
        k0 = min(_window_start(r0, rows), rows - span)
        for r in range(r0, r0 + QUERY_GROUP_ROWS):
            assert k0 <= _window_start(r, rows) and _window_start(r, rows) + win_r <= k0 + span
        groups.append((r0, k0))
    return groups, span


def _bias_kernel(rpb_ref, o_ref, *, rows):
    win_r = min(MAX_WIN_R, rows)
    lanes = rpb_ref.shape[1]
    q = lax.broadcasted_iota(jnp.int32, (GRID_W, lanes), 0)
    kc = lax.broadcasted_iota(jnp.int32, (GRID_W, lanes), 1)
    c_start = jnp.clip(q - WIN_C // 2, 0, GRID_W - WIN_C)
    valid = (kc >= c_start) & (kc < c_start + WIN_C)
    tables = []
    for dr in range(2 * MAX_WIN_R - 1):
        row = jnp.broadcast_to(rpb_ref[dr:dr + 1, :], (GRID_W, lanes))
        t = pltpu.roll(row, lanes - (WIN_C - 1), 1, stride=1, stride_axis=0)
        tables.append(jnp.where(valid, t, MASKED_SCORE)[:, :GRID_W])
    masked = jnp.full((GRID_W, GRID_W), MASKED_SCORE, F32)
    groups, span = _query_groups(rows)
    for g, (r0, k0) in enumerate(groups):
        for j in range(QUERY_GROUP_ROWS):
            start = _window_start(r0 + j, rows)
            for i in range(span):
                in_window = start <= k0 + i < start + win_r
                dr = k0 + i - (r0 + j) + (MAX_WIN_R - 1)
                o_ref[g, j * GRID_W:(j + 1) * GRID_W, i * GRID_W:(i + 1) * GRID_W] = (
                    tables[dr] if in_window else masked)


def _window_bias(rpb, rows):
    depth, n_heads, n_dr, n_dc = rpb.shape
    groups, span = _query_groups(rows)
    block = (len(groups), QUERY_GROUP_ROWS * GRID_W, span * GRID_W)
    dr_pad = -(-n_dr // SUBLANES) * SUBLANES
    rpb_rows = jnp.pad(rpb.reshape(depth * n_heads, n_dr, n_dc),
                       ((0, 0), (0, dr_pad - n_dr), (0, LANES - n_dc)))
    return pl.pallas_call(
        functools.partial(_bias_kernel, rows=rows),
        name="window_bias",
        grid=(depth * n_heads,),
        in_specs=[pl.BlockSpec((None, dr_pad, LANES), lambda i: (i, 0, 0))],
        out_specs=pl.BlockSpec((None,) + block, lambda i: (i, 0, 0, 0)),
        out_shape=jax.ShapeDtypeStruct((depth * n_heads,) + block, F32),
        compiler_params=_params(("parallel",), 32),
    )(rpb_rows)


def _lat_attn_kernel(q_ref, k_ref, v_ref, ck_ref, cv_ref, bias_ref, o_ref, *, rows, scale):
    groups, span = _query_groups(rows)
    ck = ck_ref[...].astype(BF16)
    cv = cv_ref[...].astype(BF16)
    for g, (r0, k0) in enumerate(groups):
        q_rows = slice(r0 * GRID_W, (r0 + QUERY_GROUP_ROWS) * GRID_W)
        k_rows = slice(k0 * GRID_W, (k0 + span) * GRID_W)
        q = q_ref[q_rows, :].astype(BF16)
        kw = k_ref[k_rows, :].astype(BF16)
        vw = v_ref[k_rows, :].astype(BF16)
        s_lat = _dot_nt(q, kw) * scale + bias_ref[g]
        s_ctx = _dot_nt(q, ck) * scale
        top = jnp.maximum(jnp.max(s_lat, axis=-1, keepdims=True),
                          jnp.max(s_ctx, axis=-1, keepdims=True))
        p_lat = jnp.exp(s_lat - top)
        p_ctx = jnp.exp(s_ctx - top)
        denom = jnp.sum(p_lat, axis=-1, keepdims=True) + jnp.sum(p_ctx, axis=-1, keepdims=True)
        o = _dot(p_lat.astype(BF16), vw) + _dot(p_ctx.astype(BF16), cv)
        o_ref[q_rows, :] = (o / denom).astype(o_ref.dtype)


def _lat_attention(proj, cache_k, cache_v, bias, layer, m_ctx, dec_batch, dec_seq, w_a):
    n_heads = w_a // HEAD_DIM
    rows = dec_seq // GRID_W
    past = cache_k.shape[2]
    first = m_ctx // dec_seq
    assert first * dec_seq == m_ctx
    kern = functools.partial(_lat_attn_kernel, rows=rows, scale=1.0 / math.sqrt(HEAD_DIM))
    qkv_specs = [pl.BlockSpec((dec_seq, HEAD_DIM), lambda h, b, j=j: (first + b, j * n_heads + h))
                 for j in range(3)]
    cache_spec = pl.BlockSpec((None, None, past, HEAD_DIM), lambda h, b: (b, layer, 0, h))
    return pl.pallas_call(
        kern,
        name="lat_attn",
        grid=(n_heads, dec_batch),
        in_specs=qkv_specs + [
            cache_spec, cache_spec,
            pl.BlockSpec((None,) + bias.shape[1:], lambda h, b: (layer * n_heads + h, 0, 0, 0)),
        ],
        out_specs=pl.BlockSpec((dec_seq, HEAD_DIM), lambda h, b: (b, h)),
        out_shape=jax.ShapeDtypeStruct((dec_batch * dec_seq, w_a), BF16),
        compiler_params=_params(("parallel", "parallel"), 32),
    )(proj, proj, proj, cache_k, cache_v, bias)


def _dft_tables(n):
    idx = np.arange(n, dtype=np.int64)
    ang = 2.0 * np.pi * ((idx[:, None] * idx[None, :]) % n) / n
    return np.cos(ang), np.sin(ang)


def _fourier_kernel(f_ref, wc_ref, cn_ref, sn_ref, o_ref, *, n_groups, scale):
    wc = wc_ref[...].astype(BF16)
    pc, ps = [], []
    for g in range(n_groups):
        p = _dot(f_ref[:, g * GROUP_DIM:(g + 1) * GROUP_DIM].astype(BF16), wc)
        pc.append(p[:, :GROUP_DIM])
        ps.append(p[:, GROUP_DIM:])
    y = (_dot(cn_ref[...].astype(BF16), jnp.concatenate(pc, axis=1).astype(BF16))
         - _dot(sn_ref[...].astype(BF16), jnp.concatenate(ps, axis=1).astype(BF16)))
    o_ref[...] = (y * scale).astype(o_ref.dtype)


def _fourier_operands(n_pos, w_f):
    cc, sc = _dft_tables(GROUP_DIM)
    cn, sn = _dft_tables(n_pos)
    tables = tuple(jnp.asarray(t, F32) for t in (np.concatenate([cc, sc], axis=1), cn, sn))
    specs = [pl.BlockSpec(t.shape, lambda b: (0, 0)) for t in tables]
    kern = functools.partial(_fourier_kernel, n_groups=w_f // GROUP_DIM,
                             scale=1.0 / math.sqrt(n_pos * GROUP_DIM))
    return kern, tables, specs


def _fourier(proj, col_start, w_f, n_batch, n_pos, first_block):
    col_block = col_start // w_f
    assert col_block * w_f == col_start
    kern, tables, table_specs = _fourier_operands(n_pos, w_f)
    return pl.pallas_call(
        kern,
        name="fourier",
        grid=(n_batch,),
        in_specs=[pl.BlockSpec((n_pos, w_f), lambda b: (first_block + b, col_block))] + table_specs,
        out_specs=pl.BlockSpec((n_pos, w_f), lambda b: (b, 0)),
        out_shape=jax.ShapeDtypeStruct((n_batch * n_pos, w_f), BF16),
        compiler_params=_params(("parallel",), 48),
    )(proj, *tables)


def _gelu(x):
    return 0.5 * x * (1.0 + lax.erf(x * math.sqrt(0.5)))


def _sgu_kernel(u_ref, v_ref, g_ref, w_ref, b_ref, o_ref, *, n_groups, n_chunks):
    for g in range(n_groups):
        cols = slice(g * GROUP_DIM, (g + 1) * GROUP_DIM)
        v = _rms(_gelu(v_ref[:, cols]), g_ref[:, cols]).astype(BF16)
        v_wide = jnp.concatenate([v[c * CHUNK:(c + 1) * CHUNK, :] for c in range(n_chunks)], axis=1)
        s = _dot(w_ref[g], v_wide) + b_ref[:, g:g + 1]
        for c in range(n_chunks):
            rws = slice(c * CHUNK, (c + 1) * CHUNK)
            o_ref[rws, cols] = (_gelu(u_ref[rws, cols])
                                * s[:, c * GROUP_DIM:(c + 1) * GROUP_DIM]).astype(o_ref.dtype)


def _spatial_gating(proj, col_start, w_c, g_sgu, w_sp, b_sp_t, layer):
    m = proj.shape[0]
    n_groups = w_c // GROUP_DIM
    n_chunks = 4
    tm = n_chunks * CHUNK
    col_block = col_start // w_c
    assert col_block * w_c == col_start
    kern = functools.partial(_sgu_kernel, n_groups=n_groups, n_chunks=n_chunks)
    return pl.pallas_call(
        kern,
        name="spatial_gate",
        grid=(m // tm,),
        in_specs=[
            pl.BlockSpec((tm, w_c), lambda i: (i, col_block)),
            pl.BlockSpec((tm, w_c), lambda i: (i, col_block + 1)),
            pl.BlockSpec((None, 1, w_c), lambda i: (layer, 0, 0)),
            pl.BlockSpec((None, n_groups, CHUNK, CHUNK), lambda i: (layer, 0, 0, 0)),
            pl.BlockSpec((None, CHUNK, n_groups), lambda i: (layer, 0, 0)),
        ],
        out_specs=pl.BlockSpec((tm, w_c), lambda i: (i, 0)),
        out_shape=jax.ShapeDtypeStruct((m, w_c), BF16),
        compiler_params=_params(("parallel",), 32),
    )(proj, proj, g_sgu, w_sp, b_sp_t)


def _merge_kernel(h_ref, oa_ctx_ref, oa_lat_ref, of_ctx_ref, of_lat_ref, oc_ref,
                  wga_ref, wgf_ref, wgc_ref, wa_ref, wf_ref, wc_ref, w1_ref, w2_ref,
                  o_ref, w1o_ref, w2o_ref, *, rows, tm):
    w1o_ref[...] = w1_ref[...].astype(BF16)
    w2o_ref[...] = w2_ref[...].astype(BF16)
    h = h_ref[...]
    o_a = rows.pick(tm, oa_ctx_ref, oa_lat_ref)
    o_f = rows.pick(tm, of_ctx_ref, of_lat_ref)
    acc = jax.nn.sigmoid(_dot(h, wga_ref[...])) * _dot(o_a, wa_ref[...])
    acc += jax.nn.sigmoid(_dot(h, wgf_ref[...])) * _dot(o_f, wf_ref[...])
    acc += jax.nn.sigmoid(_dot(h, wgc_ref[...])) * _dot(oc_ref[...], wc_ref[...])
    o_ref[...] = acc.astype(o_ref.dtype)


def _merge(rows, h, o_a_ctx, o_a_lat, o_f_ctx, o_f_lat, o_c, w_gates, w_br_a, w_br_f, w_br_c,
           w_mlp1, w_mlp2, layer):
    d = rows.d
    d_ff = w_mlp1.shape[2]
    tm, tn = 1024, 256
    nb = d // tn
    n_steps = (rows.m // tm) * (nb - 1)
    n_slabs = 1 << (n_steps.bit_length() - 1)
    slab1, slab2 = d // n_slabs, d_ff // n_slabs
    assert slab1 * n_slabs == d and slab2 * n_slabs == d_ff and slab1 % 16 == 0

    def slab(i, n):
        return jnp.minimum(i * (nb - 1) + jnp.minimum(n, nb - 2), n_slabs - 1)

    full = lambda width: pl.BlockSpec((tm, width), lambda i, n: (i, 0))
    gate_w = lambda j: pl.BlockSpec((d, tn), lambda i, n: (0, j * nb + n))
    br_w = lambda w: pl.BlockSpec((None, w.shape[1], tn), lambda i, n: (layer, 0, n))
    return pl.pallas_call(
        functools.partial(_merge_kernel, rows=rows, tm=tm),
        name="gate_merge",
        grid=(rows.m // tm, nb),
        in_specs=[full(d),
                  *rows.split_specs(tm, o_a_ctx.shape[1]),
                  *rows.split_specs(tm, o_f_ctx.shape[1]),
                  full(o_c.shape[1]),
                  gate_w(0), gate_w(1), gate_w(2),
                  br_w(w_br_a), br_w(w_br_f), br_w(w_br_c),
                  pl.BlockSpec((None, slab1, d_ff), lambda i, n: (layer, slab(i, n), 0)),
                  pl.BlockSpec((None, slab2, d), lambda i, n: (layer, slab(i, n), 0))],
        out_specs=[pl.BlockSpec((tm, tn), lambda i, n: (i, n)),
                   pl.BlockSpec((slab1, d_ff), lambda i, n: (slab(i, n), 0)),
                   pl.BlockSpec((slab2, d), lambda i, n: (slab(i, n), 0))],
        out_shape=[jax.ShapeDtypeStruct((rows.m, d), BF16),
                   jax.ShapeDtypeStruct((d, d_ff), BF16),
                   jax.ShapeDtypeStruct((d_ff, d), BF16)],
        compiler_params=_params(("arbitrary", "arbitrary"), 56),
    )(h, o_a_ctx, o_a_lat, o_f_ctx, o_f_lat, o_c, w_gates, w_gates, w_gates,
      w_br_a, w_br_f, w_br_c, w_mlp1, w_mlp2)


SUB_ROWS = 256


OVERLAP_PIECES = 4


def _unit_rms(y):
    return y * lax.rsqrt(jnp.mean(y * y, axis=-1, keepdims=True) + RMS_EPS)


def _outproj_kernel(mg_ref, w_ref, x_ref, gpost_ref, gt_ref, gpre_ref, sc_ref, sh_ref,
                    xo_ref, ho_ref, y_even_ref, y_odd_ref, *, n_tiles):
    s = pl.program_id(0)
    y_refs = (y_even_ref, y_odd_ref)

    def multiply(parity):
        y_refs[parity][...] = _dot(mg_ref[...], w_ref[...])

    def finish(parity, rws=slice(None)):
        post = gpost_ref[...] * gt_ref[...]
        pre = gpre_ref[...] * (1.0 + sc_ref[...])
        x1 = x_ref[rws, :] + _unit_rms(y_refs[parity][rws, :]) * post
        xo_ref[rws, :] = x1
        ho_ref[rws, :] = (_unit_rms(x1) * pre + sh_ref[...]).astype(ho_ref.dtype)
        return x1

    def multiply_and_finish(parity):
        tm, d = mg_ref.shape
        rp, cw = tm // OVERLAP_PIECES, d // OVERLAP_PIECES
        lhs = mg_ref[...]
        for j in range(OVERLAP_PIECES):
            cols = slice(j * cw, (j + 1) * cw)
            y_refs[parity][:, cols] = _dot(lhs, w_ref[:, cols])
            x1 = finish(1 - parity, slice(j * rp, (j + 1) * rp))
            lhs = mg_ref[...] + jnp.minimum(jnp.abs(x1[:1, :1]), 0.0).astype(BF16)

    @pl.when(s == 0)
    def _():
        multiply(0)

    for parity in (0, 1):
        @pl.when((s > 0) & (s < n_tiles) & (s % 2 == parity))
        def _():
            multiply_and_finish(parity)

    @pl.when(s == n_tiles)
    def _():
        finish((n_tiles - 1) % 2)


def _out_projection(rows, merged, w_out, x, g_post_mix, g_pre_mlp, mods, layer):
    tm = 512
    d = rows.d
    n_tiles = rows.m // tm
    ahead_spec = pl.BlockSpec((tm, d), lambda s: (jnp.minimum(s, n_tiles - 1), 0))
    lag_spec = pl.BlockSpec((tm, d), lambda s: (jnp.maximum(s - 1, 0), 0))
    return pl.pallas_call(
        functools.partial(_outproj_kernel, n_tiles=n_tiles),
        name="out_proj",
        grid=(n_tiles + 1,),
        in_specs=[
            ahead_spec,
            pl.BlockSpec((d, d), lambda s: (0, 0), pipeline_mode=pl.Buffered(1)),
            lag_spec,
            rows.gain_spec(layer),
            rows.mod_spec(layer, MOD_GATE1, tm, lag=1),
            rows.gain_spec(layer),
            rows.mod_spec(layer, MOD_SCALE2, tm, lag=1),
            rows.mod_spec(layer, MOD_SHIFT2, tm, lag=1),
        ],
        out_specs=[lag_spec, lag_spec],
        out_shape=[jax.ShapeDtypeStruct((rows.m, d), F32), jax.ShapeDtypeStruct((rows.m, d), BF16)],
        scratch_shapes=[pltpu.VMEM((tm, d), F32), pltpu.VMEM((tm, d), F32)],
        compiler_params=_params(("arbitrary",), 52),
    )(merged, w_out, x, g_post_mix, mods, g_pre_mlp, mods, mods)


def _mlp_kernel(*refs, n_ff_steps, has_next, rows, tm):
    if has_next:
        (h_ref, w1_ref, w2_ref, x_ref, gpost_ref, gt_ref, gpre_ref, sc_ref, sh_ref,
         wg_ref, wo_ref, xo_ref, ho_ref, wgo_ref, woo_ref, acc_ref) = refs
        wgo_ref[...] = wg_ref[...].astype(BF16)
        woo_ref[...] = wo_ref[...].astype(BF16)
    else:
        h_ref, w1_ref, w2_ref, x_ref, gpost_ref, gt_ref, yp_ref, ys_ref, acc_ref = refs
    step = pl.program_id(1)
    last = n_ff_steps - 1
    assert last >= 2

    def ff_slice(rws):
        hid = jnp.square(jnp.maximum(_dot(h_ref[rws, :], w1_ref[...]), 0.0)).astype(BF16)
        return _dot(hid, w2_ref[...])

    whole = slice(None)

    @pl.when(step == 0)
    def _():
        acc_ref[...] = ff_slice(whole)

    @pl.when((step > 0) & (step < last))
    def _():
        acc_ref[...] += ff_slice(whole)

    @pl.when(step == last)
    def _():
        post = gpost_ref[...] * gt_ref[...]
        if has_next:
            pre = gpre_ref[...] * (1.0 + sc_ref[...])
            shift = sh_ref[...]
        for c in range(tm // SUB_ROWS):
            rws = slice(c * SUB_ROWS, (c + 1) * SUB_ROWS)
            x2 = x_ref[rws, :] + _unit_rms(acc_ref[rws, :] + ff_slice(rws)) * post
            if has_next:
                xo_ref[rws, :] = x2
                ho_ref[rws, :] = (_unit_rms(x2) * pre + shift).astype(ho_ref.dtype)
            else:
                acc_ref[rws, :] = x2
        if not has_next:
            is_ctx = pl.program_id(0) < rows.m_ctx // tm

            @pl.when(is_ctx)
            def _():
                yp_ref[...] = acc_ref[...]

            @pl.when(jnp.logical_not(is_ctx))
            def _():
                ys_ref[...] = acc_ref[...]


def _mlp(rows, h2, w1, w2, x, g_post_mlp, g_pre_next, mods, w_in, col_gates, w_out, layer, has_next):
    tm, tf = 512, 1024
    d = rows.d
    d_ff = w1.shape[1]
    n_ff_steps = d_ff // tf
    row_spec = pl.BlockSpec((tm, d), lambda i, f: (i, 0))
    in_specs = [
        row_spec,
        pl.BlockSpec((d, tf), lambda i, f: (0, f)),
        pl.BlockSpec((tf, d), lambda i, f: (f, 0)),
        row_spec,
        rows.gain_spec(layer),
        rows.mod_spec(layer, MOD_GATE2, tm),
    ]
    args = [h2, w1, w2, x, g_post_mlp, mods]
    if has_next:
        in_specs += [rows.gain_spec(layer + 1),
                     rows.mod_spec(layer + 1, MOD_SCALE1, tm),
                     rows.mod_spec(layer + 1, MOD_SHIFT1, tm)]
        n_steps = (rows.m // tm) * (n_ff_steps - 1)
        n_slabs = 1 << (n_steps.bit_length() - 1)
        gate_width = w_in.shape[2] - col_gates
        piece = math.gcd(col_gates, gate_width)
        n_pieces = gate_width // piece
        g_rows, o_rows = d * n_pieces // n_slabs, d // n_slabs
        assert g_rows % 16 == 0 and o_rows % 16 == 0 and g_rows * n_slabs == d * n_pieces

        def slab(i, f):
            return jnp.minimum(i * (n_ff_steps - 1) + jnp.minimum(f, n_ff_steps - 2), n_slabs - 1)

        in_specs += [
            pl.BlockSpec((None, g_rows, piece),
                         lambda i, f: (layer + 1, slab(i, f) // n_pieces,
                                       col_gates // piece + slab(i, f) % n_pieces)),
            pl.BlockSpec((None, o_rows, d), lambda i, f: (layer + 1, slab(i, f), 0)),
        ]
        args += [g_pre_next, mods, mods, w_in, w_out]
        out_specs = [row_spec, row_spec,
                     pl.BlockSpec((g_rows, piece),
                                  lambda i, f: (slab(i, f) // n_pieces, slab(i, f) % n_pieces)),
                     pl.BlockSpec((o_rows, d), lambda i, f: (slab(i, f), 0))]
        out_shape = [jax.ShapeDtypeStruct((rows.m, d), F32),
                     jax.ShapeDtypeStruct((rows.m, d), BF16),
                     jax.ShapeDtypeStruct((d, gate_width), BF16),
                     jax.ShapeDtypeStruct((d, d), BF16)]
    else:
        out_specs = list(rows.split_specs(tm, d))
        out_shape = [jax.ShapeDtypeStruct((rows.m_ctx, d), F32),
                     jax.ShapeDtypeStruct((rows.m_lat, d), F32)]
    return pl.pallas_call(
        functools.partial(_mlp_kernel, n_ff_steps=n_ff_steps, has_next=has_next, rows=rows, tm=tm),
        name="mlp",
        grid=(rows.m // tm, n_ff_steps),
        in_specs=in_specs,
        out_specs=out_specs,
        out_shape=out_shape,
        scratch_shapes=[pltpu.VMEM((tm, d), F32)],
        compiler_params=_params(("arbitrary", "arbitrary"), 56),
    )(*args)


def kernel(x_prompt, x_sample, cache_k, cache_v, c, c_ctx, w_ada, b_ada, g_pre_mix, g_post_mix,
           g_pre_mlp, g_post_mlp, w_in, rpb, g_sgu, w_spatial, b_spatial, w_br_a, w_br_f, w_br_c,
           w_out, w_mlp1, w_mlp2):
    batch, seq, d = x_prompt.shape
    dec_batch, dec_seq, _ = x_sample.shape
    depth = w_in.shape[0]
    past = cache_k.shape[2]
    w_a = w_br_a.shape[1]
    w_f = w_br_f.shape[1]
    w_c = w_br_c.shape[1]
    n_heads = w_a // HEAD_DIM
    m_ctx = batch * seq
    rows = _Rows(m_ctx, dec_batch, dec_seq, d)
    assert dec_seq % GRID_W == 0 and m_ctx % dec_seq == 0 and MOD_ROWS >= 1 + dec_batch

    cvec = jnp.zeros((MOD_ROWS, d), F32).at[0].set(c_ctx).at[1:1 + dec_batch].set(c)
    mods = _ada(cvec, w_ada, b_ada)
    mods = mods[:, :1 + dec_batch].reshape(depth, 1 + dec_batch, N_MOD, d)
    mods = mods.transpose(0, 2, 1, 3)[:, :, :, None, :]

    col_f = 3 * w_a
    col_uv = col_f + w_f
    col_gates = col_uv + 2 * w_c

    w_gates_b = w_in[0, :, col_gates:].astype(BF16)
    w_out_b = w_out[0].astype(BF16)
    w_br_a_b, w_br_f_b, w_br_c_b = w_br_a.astype(BF16), w_br_f.astype(BF16), w_br_c.astype(BF16)
    w_sp_b = w_spatial.astype(BF16)
    b_sp_t = b_spatial.transpose(0, 2, 1)
    g_sgu_row = g_sgu.reshape(depth, 1, w_c)
    gains = [g.reshape(depth, 1, d) for g in (g_pre_mix, g_post_mix, g_pre_mlp, g_post_mlp)]
    g_pre_mix_r, g_post_mix_r, g_pre_mlp_r, g_post_mlp_r = gains
    cache_k_r = cache_k.reshape(dec_batch, depth, past, w_a)
    cache_v_r = cache_v.reshape(dec_batch, depth, past, w_a)
    bias = _window_bias(rpb, dec_seq // GRID_W)

    x, h = _prologue(rows, x_prompt.reshape(m_ctx, d), x_sample.reshape(rows.m_lat, d),
                     g_pre_mix_r, mods, 0)
    new_k = jnp.zeros((batch, depth, seq, n_heads, HEAD_DIM), F32)
    new_v = jnp.zeros((batch, depth, seq, n_heads, HEAD_DIM), F32)
    for l in range(depth):
        proj = _project(h, w_in, l, col_gates, 1536, F32)
        o_a_ctx, o_f_ctx, new_k, new_v = _ctx_mixers(proj, new_k, new_v, l, depth, batch, seq, w_a,
                                                     col_f, w_f)
        o_a_lat = _lat_attention(proj, cache_k_r, cache_v_r, bias, l, m_ctx, dec_batch, dec_seq, w_a)
        o_f_lat = _fourier(proj, col_f, w_f, dec_batch, dec_seq, m_ctx // dec_seq)
        o_c = _spatial_gating(proj, col_uv, w_c, g_sgu_row, w_sp_b, b_sp_t, l)
        merged, w1_b, w2_b = _merge(rows, h, o_a_ctx, o_a_lat, o_f_ctx, o_f_lat, o_c, w_gates_b,
                                    w_br_a_b, w_br_f_b, w_br_c_b, w_mlp1, w_mlp2, l)
        x, h2 = _out_projection(rows, merged, w_out_b, x, g_post_mix_r, g_pre_mlp_r, mods, l)
        outs = _mlp(rows, h2, w1_b, w2_b, x, g_post_mlp_r, g_pre_mix_r, mods, w_in, col_gates,
                    w_out, l, l + 1 < depth)
        if l + 1 < depth:
            x, h, w_gates_b, w_out_b = outs

    y_p = outs[0].reshape(batch, seq, d)
    y_s = outs[1].reshape(dec_batch, dec_seq, d)
    cache_shape = (batch, depth, seq, n_heads, HEAD_DIM)
    return (y_p, y_s, new_k.reshape(cache_shape), new_v.reshape(cache_shape))
```

```python
import functools
import math

import numpy as np
import jax
import jax.numpy as jnp
from jax import lax
from jax.experimental import pallas as pl
from jax.experimental.pallas import tpu as pltpu

F32 = jnp.float32
BF16 = jnp.bfloat16

RMS_EPS = 1e-6
MASKED_SCORE = -1e30

HEAD_DIM = 128
GROUP_DIM = 128
GRID_W = 64
CHUNK = 128
MAX_WIN_R = 8
WIN_C = 16
N_MOD = 6
MOD_SHIFT1, MOD_SCALE1, MOD_GATE1, MOD_SHIFT2, MOD_SCALE2, MOD_GATE2 = range(N_MOD)
LANES, SUBLANES = 128, 8
MOD_ROWS = SUBLANES

MIB = 2 ** 20


def _params(semantics, vmem_mib):
    return pltpu.CompilerParams(dimension_semantics=semantics, vmem_limit_bytes=vmem_mib * MIB)


def _rms(x, g):
    return x * lax.rsqrt(jnp.mean(x * x, axis=-1, keepdims=True) + RMS_EPS) * g


def _dot(a, b):
    return jnp.dot(a, b, preferred_element_type=F32)


def _dot_nt(a, b):
    return lax.dot_general(a, b, (((1,), (1,)), ((), ())), preferred_element_type=F32)


def _ada_kernel(c_ref, w_ref, b_ref, o_ref):
    c = c_ref[...]
    s = c * jax.nn.sigmoid(c)
    o_ref[...] = _dot(s.astype(BF16), w_ref[...].astype(BF16)) + b_ref[...]


def _ada(cvec, w_ada, b_ada):
    depth, d, width = w_ada.shape
    tn = 1024
    return pl.pallas_call(
        _ada_kernel,
        name="ada",
        grid=(depth, width // tn),
        in_specs=[
            pl.BlockSpec((MOD_ROWS, d), lambda l, n: (0, 0)),
            pl.BlockSpec((None, d, tn), lambda l, n: (l, 0, n)),
            pl.BlockSpec((None, 1, tn), lambda l, n: (l, 0, n)),
        ],
        out_specs=pl.BlockSpec((None, MOD_ROWS, tn), lambda l, n: (l, 0, n)),
        out_shape=jax.ShapeDtypeStruct((depth, MOD_ROWS, width), F32),
        compiler_params=_params(("parallel", "parallel"), 40),
    )(cvec, w_ada, b_ada.reshape(depth, 1, width))


class _Rows:
    def __init__(self, m_ctx, dec_batch, dec_seq, d):
        self.m_ctx, self.dec_batch, self.dec_seq, self.d = m_ctx, dec_batch, dec_seq, d
        self.m_lat = dec_batch * dec_seq
        self.m = m_ctx + self.m_lat

    def group(self, i, tm):
        n_ctx = self.m_ctx // tm
        per_batch = self.dec_seq // tm
        return jnp.where(i < n_ctx, 0, 1 + (i - n_ctx) // per_batch)

    def mod_spec(self, layer, piece, tm, lag=0):
        return pl.BlockSpec((None, None, None, 1, self.d),
                            lambda i, *_: (layer, piece, self.group(jnp.maximum(i - lag, 0), tm), 0, 0))

    def gain_spec(self, layer):
        return pl.BlockSpec((None, 1, self.d), lambda i, *_: (layer, 0, 0))

    def split_specs(self, tm, width):
        n_ctx = self.m_ctx // tm
        ctx = pl.BlockSpec((tm, width), lambda i, *_: (jnp.minimum(i, n_ctx - 1), 0))
        lat = pl.BlockSpec((tm, width), lambda i, *_: (jnp.maximum(i - n_ctx, 0), 0))
        return ctx, lat

    def pick(self, tm, ctx_ref, lat_ref):
        return jnp.where(pl.program_id(0) < self.m_ctx // tm, ctx_ref[...], lat_ref[...])


def _prologue_kernel(xp_ref, xs_ref, g_ref, sc_ref, sh_ref, x_ref, h_ref, *, rows, tm):
    x = rows.pick(tm, xp_ref, xs_ref)
    x_ref[...] = x
    h = _rms(x, g_ref[...]) * (1.0 + sc_ref[...]) + sh_ref[...]
    h_ref[...] = h.astype(h_ref.dtype)


def _prologue(rows, x_ctx, x_lat, g_pre, mods, layer):
    tm = 512
    row_spec = pl.BlockSpec((tm, rows.d), lambda i: (i, 0))
    return pl.pallas_call(
        functools.partial(_prologue_kernel, rows=rows, tm=tm),
        name="prologue",
        grid=(rows.m // tm,),
        in_specs=[
            *rows.split_specs(tm, rows.d),
            rows.gain_spec(layer),
            rows.mod_spec(layer, MOD_SCALE1, tm),
            rows.mod_spec(layer, MOD_SHIFT1, tm),
        ],
        out_specs=[row_spec, row_spec],
        out_shape=[jax.ShapeDtypeStruct((rows.m, rows.d), F32),
                   jax.ShapeDtypeStruct((rows.m, rows.d), BF16)],
        compiler_params=_params(("arbitrary",), 40),
    )(x_ctx, x_lat, g_pre, mods, mods)


def _inproj_kernel(a_ref, w_ref, o_ref, wb_ref):
    @pl.when(pl.program_id(1) == 0)
    def _():
        wb_ref[...] = w_ref[...].astype(BF16)

    o_ref[...] = _dot(a_ref[...], wb_ref[...]).astype(o_ref.dtype)


def _project(h, w, layer, width, tn, out_dtype):
    m, k = h.shape
    tm = 1024
    assert width % tn == 0
    return pl.pallas_call(
        _inproj_kernel,
        name="in_proj",
        grid=(width // tn, m // tm),
        in_specs=[
            pl.BlockSpec((tm, k), lambda n, i: (i, 0)),
            pl.BlockSpec((None, k, tn), lambda n, i: (layer, 0, n)),
        ],
        out_specs=pl.BlockSpec((tm, tn), lambda n, i: (i, n)),
        out_shape=jax.ShapeDtypeStruct((m, width), out_dtype),
        scratch_shapes=[pltpu.VMEM((k, tn), BF16)],
        compiler_params=_params(("arbitrary", "arbitrary"), 58),
    )(h, w)


CTX_BATCHES_PER_STEP = 2


def _ctx_attn_kernel(q_ref, k_ref, v_ref, o_ref, ko_ref, vo_ref, *, n_heads, scale):
    ko_ref[...] = k_ref[...].reshape(ko_ref.shape)
    vo_ref[...] = v_ref[...].reshape(vo_ref.shape)
    for h in range(n_heads):
        cols = slice(h * HEAD_DIM, (h + 1) * HEAD_DIM)
        q = (q_ref[:, cols] * scale).astype(BF16)
        k = k_ref[:, cols].astype(BF16)
        v = v_ref[:, cols].astype(BF16)
        s = _dot_nt(q, k)
        p = jnp.exp(s - jnp.max(s, axis=-1, keepdims=True))
        denom = jnp.sum(p, axis=-1, keepdims=True)
        o_ref[:, cols] = (_dot(p.astype(BF16), v) / denom).astype(o_ref.dtype)


def _ctx_mixers(proj, new_k, new_v, layer, depth, batch, seq, w_a, col_f, w_f):
    n_heads = w_a // HEAD_DIM
    attend = functools.partial(_ctx_attn_kernel, n_heads=n_heads, scale=1.0 / math.sqrt(HEAD_DIM))
    fourier, tables, table_specs = _fourier_operands(seq, w_f)
    f_block = col_f // w_f
    assert f_block * w_f == col_f
    bt = CTX_BATCHES_PER_STEP
    assert batch % bt == 0
    cache_spec = pl.BlockSpec((bt, None, seq, n_heads, HEAD_DIM), lambda b: (b, layer, 0, 0, 0))
    cache_shape = jax.ShapeDtypeStruct((batch, depth, seq, n_heads, HEAD_DIM), F32)
    in_specs = [pl.BlockSpec((bt * seq, w_a), lambda b, j=j: (b, j)) for j in range(3)]
    in_specs += [pl.BlockSpec((bt * seq, w_f), lambda b: (b, f_block))] + table_specs
    in_specs += [pl.BlockSpec(memory_space=pl.ANY)] * 2
    n_in = len(in_specs)

    def body(q_ref, k_ref, v_ref, f_ref, wc_ref, cn_ref, sn_ref, k_all_ref, v_all_ref,
             o_ref, of_ref, ko_ref, vo_ref):
        del k_all_ref, v_all_ref
        for t in range(bt):
            rws = pl.ds(t * seq, seq)
            attend(q_ref.at[rws], k_ref.at[rws], v_ref.at[rws], o_ref.at[rws],
                   ko_ref.at[t], vo_ref.at[t])
            fourier(f_ref.at[rws], wc_ref, cn_ref, sn_ref, of_ref.at[rws])

    return pl.pallas_call(
        body,
        name="ctx_mixers",
        grid=(batch // bt,),
        in_specs=in_specs,
        out_specs=[pl.BlockSpec((bt * seq, w_a), lambda b: (b, 0)),
                   pl.BlockSpec((bt * seq, w_f), lambda b: (b, 0)), cache_spec, cache_spec],
        out_shape=[jax.ShapeDtypeStruct((batch * seq, w_a), BF16),
                   jax.ShapeDtypeStruct((batch * seq, w_f), BF16), cache_shape, cache_shape],
        input_output_aliases={n_in - 2: 2, n_in - 1: 3},
        compiler_params=_params(("arbitrary",), 32),
    )(proj, proj, proj, proj, *tables, new_k, new_v)


def _window_start(r, rows):
    win_r = min(MAX_WIN_R, rows)
    return min(max(r - win_r // 2, 0), rows - win_r)


QUERY_GROUP_ROWS = 4


def _query_groups(rows):
    win_r = min(MAX_WIN_R, rows)
    span = min(rows, win_r + QUERY_GROUP_ROWS)
    assert rows % QUERY_GROUP_ROWS == 0
    groups = []
    for r0 in range(0, rows, QUERY_GROUP_ROWS):
        k0 = min(_window_start(r0, rows), rows - span)
        for r in range(r0, r0 + QUERY_GROUP_ROWS):
            assert k0 <= _window_start(r, rows) and _window_start(r, rows) + win_r <= k0 + span
        groups.append((r0, k0))
    return groups, span


def _bias_kernel(rpb_ref, o_ref, *, rows):
    win_r = min(MAX_WIN_R, rows)
    lanes = rpb_ref.shape[1]
    q = lax.broadcasted_iota(jnp.int32, (GRID_W, lanes), 0)
    kc = lax.broadcasted_iota(jnp.int32, (GRID_W, lanes), 1)
    c_start = jnp.clip(q - WIN_C // 2, 0, GRID_W - WIN_C)
    valid = (kc >= c_start) & (kc < c_start + WIN_C)
    tables = []
    for dr in range(2 * MAX_WIN_R - 1):
        row = jnp.broadcast_to(rpb_ref[dr:dr + 1, :], (GRID_W, lanes))
        t = pltpu.roll(row, lanes - (WIN_C - 1), 1, stride=1, stride_axis=0)
        tables.append(jnp.where(valid, t, MASKED_SCORE)[:, :GRID_W])
    masked = jnp.full((GRID_W, GRID_W), MASKED_SCORE, F32)
    groups, span = _query_groups(rows)
    for g, (r0, k0) in enumerate(groups):
        for j in range(QUERY_GROUP_ROWS):
            start = _window_start(r0 + j, rows)
            for i in range(span):
                in_window = start <= k0 + i < start + win_r
                dr = k0 + i - (r0 + j) + (MAX_WIN_R - 1)
                o_ref[g, j * GRID_W:(j + 1) * GRID_W, i * GRID_W:(i + 1) * GRID_W] = (
                    tables[dr] if in_window else masked)


def _window_bias(rpb, rows):
    depth, n_heads, n_dr, n_dc = rpb.shape
    groups, span = _query_groups(rows)
    block = (len(groups), QUERY_GROUP_ROWS * GRID_W, span * GRID_W)
    dr_pad = -(-n_dr // SUBLANES) * SUBLANES
    rpb_rows = jnp.pad(rpb.reshape(depth * n_heads, n_dr, n_dc),
                       ((0, 0), (0, dr_pad - n_dr), (0, LANES - n_dc)))
    return pl.pallas_call(
        functools.partial(_bias_kernel, rows=rows),
        name="window_bias",
        grid=(depth * n_heads,),
        in_specs=[pl.BlockSpec((None, dr_pad, LANES), lambda i: (i, 0, 0))],
        out_specs=pl.BlockSpec((None,) + block, lambda i: (i, 0, 0, 0)),
        out_shape=jax.ShapeDtypeStruct((depth * n_heads,) + block, F32),
        compiler_params=_params(("parallel",), 32),
    )(rpb_rows)


def _lat_attn_kernel(q_ref, k_ref, v_ref, ck_ref, cv_ref, bias_ref, o_ref, *, rows, scale):
    groups, span = _query_groups(rows)
    ck = ck_ref[...].astype(BF16)
    cv = cv_ref[...].astype(BF16)
    for g, (r0, k0) in enumerate(groups):
        q_rows = slice(r0 * GRID_W, (r0 + QUERY_GROUP_ROWS) * GRID_W)
        k_rows = slice(k0 * GRID_W, (k0 + span) * GRID_W)
        q = (q_ref[q_rows, :] * scale).astype(BF16)
        kw = k_ref[k_rows, :].astype(BF16)
        vw = v_ref[k_rows, :].astype(BF16)
        s_lat = _dot_nt(q, kw) + bias_ref[g]
        s_ctx = _dot_nt(q, ck)
        top = jnp.maximum(jnp.max(s_lat, axis=-1, keepdims=True),
                          jnp.max(s_ctx, axis=-1, keepdims=True))
        p_lat = jnp.exp(s_lat - top)
        p_ctx = jnp.exp(s_ctx - top)
        denom = jnp.sum(p_lat, axis=-1, keepdims=True) + jnp.sum(p_ctx, axis=-1, keepdims=True)
        o = _dot(p_lat.astype(BF16), vw) + _dot(p_ctx.astype(BF16), cv)
        o_ref[q_rows, :] = (o / denom).astype(o_ref.dtype)


def _lat_attention(proj, cache_k, cache_v, bias, layer, m_ctx, dec_batch, dec_seq, w_a):
    n_heads = w_a // HEAD_DIM
    rows = dec_seq // GRID_W
    past = cache_k.shape[2]
    first = m_ctx // dec_seq
    assert first * dec_seq == m_ctx
    kern = functools.partial(_lat_attn_kernel, rows=rows, scale=1.0 / math.sqrt(HEAD_DIM))
    qkv_specs = [pl.BlockSpec((dec_seq, HEAD_DIM), lambda h, b, j=j: (first + b, j * n_heads + h))
                 for j in range(3)]
    cache_spec = pl.BlockSpec((None, None, past, HEAD_DIM), lambda h, b: (b, layer, 0, h))
    return pl.pallas_call(
        kern,
        name="lat_attn",
        grid=(n_heads, dec_batch),
        in_specs=qkv_specs + [
            cache_spec, cache_spec,
            pl.BlockSpec((None,) + bias.shape[1:], lambda h, b: (layer * n_heads + h, 0, 0, 0)),
        ],
        out_specs=pl.BlockSpec((dec_seq, HEAD_DIM), lambda h, b: (b, h)),
        out_shape=jax.ShapeDtypeStruct((dec_batch * dec_seq, w_a), BF16),
        compiler_params=_params(("parallel", "parallel"), 32),
    )(proj, proj, proj, cache_k, cache_v, bias)


def _dft_tables(n):
    idx = np.arange(n, dtype=np.int64)
    ang = 2.0 * np.pi * ((idx[:, None] * idx[None, :]) % n) / n
    return np.cos(ang), np.sin(ang)


def _fourier_kernel(f_ref, wc_ref, cn_ref, sn_ref, o_ref, *, n_groups, scale):
    wc = wc_ref[...].astype(BF16)
    pc, ps = [], []
    for g in range(n_groups):
        p = _dot(f_ref[:, g * GROUP_DIM:(g + 1) * GROUP_DIM].astype(BF16), wc)
        pc.append(p[:, :GROUP_DIM])
        ps.append(p[:, GROUP_DIM:])
    y = (_dot(cn_ref[...].astype(BF16), jnp.concatenate(pc, axis=1).astype(BF16))
         - _dot(sn_ref[...].astype(BF16), jnp.concatenate(ps, axis=1).astype(BF16)))
    o_ref[...] = (y * scale).astype(o_ref.dtype)


def _fourier_operands(n_pos, w_f):
    cc, sc = _dft_tables(GROUP_DIM)
    cn, sn = _dft_tables(n_pos)
    tables = tuple(jnp.asarray(t, F32) for t in (np.concatenate([cc, sc], axis=1), cn, sn))
    specs = [pl.BlockSpec(t.shape, lambda b: (0, 0)) for t in tables]
    kern = functools.partial(_fourier_kernel, n_groups=w_f // GROUP_DIM,
                             scale=1.0 / math.sqrt(n_pos * GROUP_DIM))
    return kern, tables, specs


def _fourier(proj, col_start, w_f, n_batch, n_pos, first_block):
    col_block = col_start // w_f
    assert col_block * w_f == col_start
    kern, tables, table_specs = _fourier_operands(n_pos, w_f)
    return pl.pallas_call(
        kern,
        name="fourier",
        grid=(n_batch,),
        in_specs=[pl.BlockSpec((n_pos, w_f), lambda b: (first_block + b, col_block))] + table_specs,
        out_specs=pl.BlockSpec((n_pos, w_f), lambda b: (b, 0)),
        out_shape=jax.ShapeDtypeStruct((n_batch * n_pos, w_f), BF16),
        compiler_params=_params(("parallel",), 48),
    )(proj, *tables)


def _gelu(x):
    return 0.5 * x * (1.0 + lax.erf(x * math.sqrt(0.5)))


def _sgu_kernel(u_ref, v_ref, g_ref, w_ref, b_ref, o_ref, *, n_groups, n_chunks):
    for g in range(n_groups):
        cols = slice(g * GROUP_DIM, (g + 1) * GROUP_DIM)
        v = _rms(_gelu(v_ref[:, cols]), g_ref[:, cols]).astype(BF16)
        v_wide = jnp.concatenate([v[c * CHUNK:(c + 1) * CHUNK, :] for c in range(n_chunks)], axis=1)
        s = _dot(w_ref[g], v_wide) + b_ref[:, g:g + 1]
        for c in range(n_chunks):
            rws = slice(c * CHUNK, (c + 1) * CHUNK)
            o_ref[rws, cols] = (_gelu(u_ref[rws, cols])
                                * s[:, c * GROUP_DIM:(c + 1) * GROUP_DIM]).astype(o_ref.dtype)


def _spatial_gating(proj, col_start, w_c, g_sgu, w_sp, b_sp_t, layer):
    m = proj.shape[0]
    n_groups = w_c // GROUP_DIM
    n_chunks = 8
    tm = n_chunks * CHUNK
    col_block = col_start // w_c
    assert col_block * w_c == col_start
    kern = functools.partial(_sgu_kernel, n_groups=n_groups, n_chunks=n_chunks)
    return pl.pallas_call(
        kern,
        name="spatial_gate",
        grid=(m // tm,),
        in_specs=[
            pl.BlockSpec((tm, w_c), lambda i: (i, col_block)),
            pl.BlockSpec((tm, w_c), lambda i: (i, col_block + 1)),
            pl.BlockSpec((None, 1, w_c), lambda i: (layer, 0, 0)),
            pl.BlockSpec((None, n_groups, CHUNK, CHUNK), lambda i: (layer, 0, 0, 0)),
            pl.BlockSpec((None, CHUNK, n_groups), lambda i: (layer, 0, 0)),
        ],
        out_specs=pl.BlockSpec((tm, w_c), lambda i: (i, 0)),
        out_shape=jax.ShapeDtypeStruct((m, w_c), BF16),
        compiler_params=_params(("parallel",), 32),
    )(proj, proj, g_sgu, w_sp, b_sp_t)


def _merge_kernel(h_ref, oa_ctx_ref, oa_lat_ref, of_ctx_ref, of_lat_ref, oc_ref,
                  wga_ref, wgf_ref, wgc_ref, wa_ref, wf_ref, wc_ref, w1_ref, w2_ref,
                  o_ref, w1o_ref, w2o_ref, *, rows, tm):
    w1o_ref[...] = w1_ref[...].astype(BF16)
    w2o_ref[...] = w2_ref[...].astype(BF16)
    h = h_ref[...]
    o_a = rows.pick(tm, oa_ctx_ref, oa_lat_ref)
    o_f = rows.pick(tm, of_ctx_ref, of_lat_ref)
    acc = jax.nn.sigmoid(_dot(h, wga_ref[...])) * _dot(o_a, wa_ref[...])
    acc += jax.nn.sigmoid(_dot(h, wgf_ref[...])) * _dot(o_f, wf_ref[...])
    acc += jax.nn.sigmoid(_dot(h, wgc_ref[...])) * _dot(oc_ref[...], wc_ref[...])
    o_ref[...] = acc.astype(o_ref.dtype)


def _merge(rows, h, o_a_ctx, o_a_lat, o_f_ctx, o_f_lat, o_c, w_gates, w_br_a, w_br_f, w_br_c,
           w_mlp1, w_mlp2, layer):
    d = rows.d
    d_ff = w_mlp1.shape[2]
    tm, tn = 1024, 256
    nb = d // tn
    n_steps = (rows.m // tm) * (nb - 1)
    n_slabs = 1 << (n_steps.bit_length() - 1)
    slab1, slab2 = d // n_slabs, d_ff // n_slabs
    assert slab1 * n_slabs == d and slab2 * n_slabs == d_ff and slab1 % 16 == 0

    def slab(i, n):
        return jnp.minimum(i * (nb - 1) + jnp.minimum(n, nb - 2), n_slabs - 1)

    full = lambda width: pl.BlockSpec((tm, width), lambda i, n: (i, 0))
    gate_w = lambda j: pl.BlockSpec((d, tn), lambda i, n: (0, j * nb + n))
    br_w = lambda w: pl.BlockSpec((None, w.shape[1], tn), lambda i, n: (layer, 0, n))
    return pl.pallas_call(
        functools.partial(_merge_kernel, rows=rows, tm=tm),
        name="gate_merge",
        grid=(rows.m // tm, nb),
        in_specs=[full(d),
                  *rows.split_specs(tm, o_a_ctx.shape[1]),
                  *rows.split_specs(tm, o_f_ctx.shape[1]),
                  full(o_c.shape[1]),
                  gate_w(0), gate_w(1), gate_w(2),
                  br_w(w_br_a), br_w(w_br_f), br_w(w_br_c),
                  pl.BlockSpec((None, slab1, d_ff), lambda i, n: (layer, slab(i, n), 0)),
                  pl.BlockSpec((None, slab2, d), lambda i, n: (layer, slab(i, n), 0))],
        out_specs=[pl.BlockSpec((tm, tn), lambda i, n: (i, n)),
                   pl.BlockSpec((slab1, d_ff), lambda i, n: (slab(i, n), 0)),
                   pl.BlockSpec((slab2, d), lambda i, n: (slab(i, n), 0))],
        out_shape=[jax.ShapeDtypeStruct((rows.m, d), BF16),
                   jax.ShapeDtypeStruct((d, d_ff), BF16),
                   jax.ShapeDtypeStruct((d_ff, d), BF16)],
        compiler_params=_params(("arbitrary", "arbitrary"), 56),
    )(h, o_a_ctx, o_a_lat, o_f_ctx, o_f_lat, o_c, w_gates, w_gates, w_gates,
      w_br_a, w_br_f, w_br_c, w_mlp1, w_mlp2)


SUB_ROWS = 256


OVERLAP_PIECES = 4


def _unit_rms(y):
    return y * lax.rsqrt(jnp.mean(y * y, axis=-1, keepdims=True) + RMS_EPS)


def _outproj_kernel(mg_ref, w_ref, x_ref, gpost_ref, gt_ref, gpre_ref, sc_ref, sh_ref,
                    xo_ref, ho_ref, y_even_ref, y_odd_ref, *, n_tiles):
    s = pl.program_id(0)
    y_refs = (y_even_ref, y_odd_ref)

    def multiply(parity):
        y_refs[parity][...] = _dot(mg_ref[...], w_ref[...])

    def finish(parity, rws=slice(None)):
        post = gpost_ref[...] * gt_ref[...]
        pre = gpre_ref[...] * (1.0 + sc_ref[...])
        x1 = x_ref[rws, :] + _unit_rms(y_refs[parity][rws, :]) * post
        xo_ref[rws, :] = x1
        ho_ref[rws, :] = (_unit_rms(x1) * pre + sh_ref[...]).astype(ho_ref.dtype)
        return x1

    def multiply_and_finish(parity):
        tm, d = mg_ref.shape
        rp, cw = tm // OVERLAP_PIECES, d // OVERLAP_PIECES
        lhs = mg_ref[...]
        for j in range(OVERLAP_PIECES):
            cols = slice(j * cw, (j + 1) * cw)
            y_refs[parity][:, cols] = _dot(lhs, w_ref[:, cols])
            x1 = finish(1 - parity, slice(j * rp, (j + 1) * rp))
            lhs = mg_ref[...] + jnp.minimum(jnp.abs(x1[:1, :1]), 0.0).astype(BF16)

    @pl.when(s == 0)
    def _():
        multiply(0)

    for parity in (0, 1):
        @pl.when((s > 0) & (s < n_tiles) & (s % 2 == parity))
        def _():
            multiply_and_finish(parity)

    @pl.when(s == n_tiles)
    def _():
        finish((n_tiles - 1) % 2)


def _out_projection(rows, merged, w_out, x, g_post_mix, g_pre_mlp, mods, layer):
    tm = 512
    d = rows.d
    n_tiles = rows.m // tm
    ahead_spec = pl.BlockSpec((tm, d), lambda s: (jnp.minimum(s, n_tiles - 1), 0))
    lag_spec = pl.BlockSpec((tm, d), lambda s: (jnp.maximum(s - 1, 0), 0))
    return pl.pallas_call(
        functools.partial(_outproj_kernel, n_tiles=n_tiles),
        name="out_proj",
        grid=(n_tiles + 1,),
        in_specs=[
            ahead_spec,
            pl.BlockSpec((d, d), lambda s: (0, 0), pipeline_mode=pl.Buffered(1)),
            lag_spec,
            rows.gain_spec(layer),
            rows.mod_spec(layer, MOD_GATE1, tm, lag=1),
            rows.gain_spec(layer),
            rows.mod_spec(layer, MOD_SCALE2, tm, lag=1),
            rows.mod_spec(layer, MOD_SHIFT2, tm, lag=1),
        ],
        out_specs=[lag_spec, lag_spec],
        out_shape=[jax.ShapeDtypeStruct((rows.m, d), F32), jax.ShapeDtypeStruct((rows.m, d), BF16)],
        scratch_shapes=[pltpu.VMEM((tm, d), F32), pltpu.VMEM((tm, d), F32)],
        compiler_params=_params(("arbitrary",), 52),
    )(merged, w_out, x, g_post_mix, mods, g_pre_mlp, mods, mods)


def _mlp_kernel(*refs, n_ff_steps, has_next, rows, tm):
    if has_next:
        (h_ref, w1_ref, w2_ref, x_ref, gpost_ref, gt_ref, gpre_ref, sc_ref, sh_ref,
         wg_ref, wo_ref, xo_ref, ho_ref, wgo_ref, woo_ref, acc_ref) = refs
        wgo_ref[...] = wg_ref[...].astype(BF16)
        woo_ref[...] = wo_ref[...].astype(BF16)
    else:
        h_ref, w1_ref, w2_ref, x_ref, gpost_ref, gt_ref, yp_ref, ys_ref, acc_ref = refs
    step = pl.program_id(1)
    last = n_ff_steps - 1
    assert last >= 2

    def ff_slice(rws):
        hid = jnp.square(jnp.maximum(_dot(h_ref[rws, :], w1_ref[...]), 0.0)).astype(BF16)
        return _dot(hid, w2_ref[...])

    whole = slice(None)

    @pl.when(step == 0)
    def _():
        acc_ref[...] = ff_slice(whole)

    @pl.when((step > 0) & (step < last))
    def _():
        acc_ref[...] += ff_slice(whole)

    @pl.when(step == last)
    def _():
        post = gpost_ref[...] * gt_ref[...]
        if has_next:
            pre = gpre_ref[...] * (1.0 + sc_ref[...])
            shift = sh_ref[...]
        for c in range(tm // SUB_ROWS):
            rws = slice(c * SUB_ROWS, (c + 1) * SUB_ROWS)
            x2 = x_ref[rws, :] + _unit_rms(acc_ref[rws, :] + ff_slice(rws)) * post
            if has_next:
                xo_ref[rws, :] = x2
                ho_ref[rws, :] = (_unit_rms(x2) * pre + shift).astype(ho_ref.dtype)
            else:
                acc_ref[rws, :] = x2
        if not has_next:
            is_ctx = pl.program_id(0) < rows.m_ctx // tm

            @pl.when(is_ctx)
            def _():
                yp_ref[...] = acc_ref[...]

            @pl.when(jnp.logical_not(is_ctx))
            def _():
                ys_ref[...] = acc_ref[...]


def _mlp(rows, h2, w1, w2, x, g_post_mlp, g_pre_next, mods, w_in, col_gates, w_out, layer, has_next):
    tm, tf = 512, 1024
    d = rows.d
    d_ff = w1.shape[1]
    n_ff_steps = d_ff // tf
    row_spec = pl.BlockSpec((tm, d), lambda i, f: (i, 0))
    in_specs = [
        row_spec,
        pl.BlockSpec((d, tf), lambda i, f: (0, f)),
        pl.BlockSpec((tf, d), lambda i, f: (f, 0)),
        row_spec,
        rows.gain_spec(layer),
        rows.mod_spec(layer, MOD_GATE2, tm),
    ]
    args = [h2, w1, w2, x, g_post_mlp, mods]
    if has_next:
        in_specs += [rows.gain_spec(layer + 1),
                     rows.mod_spec(layer + 1, MOD_SCALE1, tm),
                     rows.mod_spec(layer + 1, MOD_SHIFT1, tm)]
        n_steps = (rows.m // tm) * (n_ff_steps - 1)
        n_slabs = 1 << (n_steps.bit_length() - 1)
        gate_width = w_in.shape[2] - col_gates
        piece = math.gcd(col_gates, gate_width)
        n_pieces = gate_width // piece
        g_rows, o_rows = d * n_pieces // n_slabs, d // n_slabs
        assert g_rows % 16 == 0 and o_rows % 16 == 0 and g_rows * n_slabs == d * n_pieces

        def slab(i, f):
            return jnp.minimum(i * (n_ff_steps - 1) + jnp.minimum(f, n_ff_steps - 2), n_slabs - 1)

        in_specs += [
            pl.BlockSpec((None, g_rows, piece),
                         lambda i, f: (layer + 1, slab(i, f) // n_pieces,
                                       col_gates // piece + slab(i, f) % n_pieces)),
            pl.BlockSpec((None, o_rows, d), lambda i, f: (layer + 1, slab(i, f), 0)),
        ]
        args += [g_pre_next, mods, mods, w_in, w_out]
        out_specs = [row_spec, row_spec,
                     pl.BlockSpec((g_rows, piece),
                                  lambda i, f: (slab(i, f) // n_pieces, slab(i, f) % n_pieces)),
                     pl.BlockSpec((o_rows, d), lambda i, f: (slab(i, f), 0))]
        out_shape = [jax.ShapeDtypeStruct((rows.m, d), F32),
                     jax.ShapeDtypeStruct((rows.m, d), BF16),
                     jax.ShapeDtypeStruct((d, gate_width), BF16),
                     jax.ShapeDtypeStruct((d, d), BF16)]
    else:
        out_specs = list(rows.split_specs(tm, d))
        out_shape = [jax.ShapeDtypeStruct((rows.m_ctx, d), F32),
                     jax.ShapeDtypeStruct((rows.m_lat, d), F32)]
    return pl.pallas_call(
        functools.partial(_mlp_kernel, n_ff_steps=n_ff_steps, has_next=has_next, rows=rows, tm=tm),
        name="mlp",
        grid=(rows.m // tm, n_ff_steps),
        in_specs=in_specs,
        out_specs=out_specs,
        out_shape=out_shape,
        scratch_shapes=[pltpu.VMEM((tm, d), F32)],
        compiler_params=_params(("arbitrary", "arbitrary"), 56),
    )(*args)


def kernel(x_prompt, x_sample, cache_k, cache_v, c, c_ctx, w_ada, b_ada, g_pre_mix, g_post_mix,
           g_pre_mlp, g_post_mlp, w_in, rpb, g_sgu, w_spatial, b_spatial, w_br_a, w_br_f, w_br_c,
           w_out, w_mlp1, w_mlp2):
    batch, seq, d = x_prompt.shape
    dec_batch, dec_seq, _ = x_sample.shape
    depth = w_in.shape[0]
    past = cache_k.shape[2]
    w_a = w_br_a.shape[1]
    w_f = w_br_f.shape[1]
    w_c = w_br_c.shape[1]
    n_heads = w_a // HEAD_DIM
    m_ctx = batch * seq
    rows = _Rows(m_ctx, dec_batch, dec_seq, d)
    assert dec_seq % GRID_W == 0 and m_ctx % dec_seq == 0 and MOD_ROWS >= 1 + dec_batch

    cvec = jnp.zeros((MOD_ROWS, d), F32).at[0].set(c_ctx).at[1:1 + dec_batch].set(c)
    mods = _ada(cvec, w_ada, b_ada)
    mods = mods[:, :1 + dec_batch].reshape(depth, 1 + dec_batch, N_MOD, d)
    mods = mods.transpose(0, 2, 1, 3)[:, :, :, None, :]

    col_f = 3 * w_a
    col_uv = col_f + w_f
    col_gates = col_uv + 2 * w_c

    w_gates_b = w_in[0, :, col_gates:].astype(BF16)
    w_out_b = w_out[0].astype(BF16)
    w_br_a_b, w_br_f_b, w_br_c_b = w_br_a.astype(BF16), w_br_f.astype(BF16), w_br_c.astype(BF16)
    w_sp_b = w_spatial.astype(BF16)
    b_sp_t = b_spatial.transpose(0, 2, 1)
    g_sgu_row = g_sgu.reshape(depth, 1, w_c)
    gains = [g.reshape(depth, 1, d) for g in (g_pre_mix, g_post_mix, g_pre_mlp, g_post_mlp)]
    g_pre_mix_r, g_post_mix_r, g_pre_mlp_r, g_post_mlp_r = gains
    cache_k_r = cache_k.reshape(dec_batch, depth, past, w_a)
    cache_v_r = cache_v.reshape(dec_batch, depth, past, w_a)
    bias = _window_bias(rpb, dec_seq // GRID_W)

    x, h = _prologue(rows, x_prompt.reshape(m_ctx, d), x_sample.reshape(rows.m_lat, d),
                     g_pre_mix_r, mods, 0)
    new_k = jnp.zeros((batch, depth, seq, n_heads, HEAD_DIM), F32)
    new_v = jnp.zeros((batch, depth, seq, n_heads, HEAD_DIM), F32)
    for l in range(depth):
        proj = _project(h, w_in, l, col_gates, 1536, F32)
        o_a_ctx, o_f_ctx, new_k, new_v = _ctx_mixers(proj, new_k, new_v, l, depth, batch, seq, w_a,
                                                     col_f, w_f)
        o_a_lat = _lat_attention(proj, cache_k_r, cache_v_r, bias, l, m_ctx, dec_batch, dec_seq, w_a)
        o_f_lat = _fourier(proj, col_f, w_f, dec_batch, dec_seq, m_ctx // dec_seq)
        o_c = _spatial_gating(proj, col_uv, w_c, g_sgu_row, w_sp_b, b_sp_t, l)
        merged, w1_b, w2_b = _merge(rows, h, o_a_ctx, o_a_lat, o_f_ctx, o_f_lat, o_c, w_gates_b,
                                    w_br_a_b, w_br_f_b, w_br_c_b, w_mlp1, w_mlp2, l)
        x, h2 = _out_projection(rows, merged, w_out_b, x, g_post_mix_r, g_pre_mlp_r, mods, l)
        outs = _mlp(rows, h2, w1_b, w2_b, x, g_post_mlp_r, g_pre_mix_r, mods, w_in, col_gates,
                    w_out, l, l + 1 < depth)
        if l + 1 < depth:
            x, h, w_gates_b, w_out_b = outs

    y_p = outs[0].reshape(batch, seq, d)
    y_s = outs[1].reshape(dec_batch, dec_seq, d)
    cache_shape = (batch, depth, seq, n_heads, HEAD_DIM)
    return (y_p, y_s, new_k.reshape(cache_shape), new_v.reshape(cache_shape))
```

```python
import functools
import math

import numpy as np
import jax
import jax.numpy as jnp
from jax import lax
from jax.experimental import pallas as pl
from jax.experimental.pallas import tpu as pltpu

F32 = jnp.float32
BF16 = jnp.bfloat16

RMS_EPS = 1e-6
MASKED_SCORE = -math.inf

HEAD_DIM = 128
GROUP_DIM = 128
GRID_W = 64
CHUNK = 128
MAX_WIN_R = 8
WIN_C = 16
N_MOD = 6
MOD_SHIFT1, MOD_SCALE1, MOD_GATE1, MOD_SHIFT2, MOD_SCALE2, MOD_GATE2 = range(N_MOD)
LANES, SUBLANES = 128, 8
MOD_ROWS = SUBLANES

MIB = 2 ** 20


def _params(semantics, vmem_mib):
    return pltpu.CompilerParams(dimension_semantics=semantics, vmem_limit_bytes=vmem_mib * MIB)


def _rms(x, g):
    return x * lax.rsqrt(jnp.mean(x * x, axis=-1, keepdims=True) + RMS_EPS) * g


def _dot(a, b):
    return jnp.dot(a, b, preferred_element_type=F32)


def _dot_nt(a, b):
    return lax.dot_general(a, b, (((1,), (1,)), ((), ())), preferred_element_type=F32)


def _ada_kernel(c_ref, w_ref, b_ref, o_ref):
    c = c_ref[...]
    s = c * jax.nn.sigmoid(c)
    o_ref[...] = _dot(s.astype(BF16), w_ref[...].astype(BF16)) + b_ref[...]


def _ada(cvec, w_ada, b_ada):
    depth, d, width = w_ada.shape
    tn = 1024
    return pl.pallas_call(
        _ada_kernel,
        name="ada",
        grid=(depth, width // tn),
        in_specs=[
            pl.BlockSpec((MOD_ROWS, d), lambda l, n: (0, 0)),
            pl.BlockSpec((None, d, tn), lambda l, n: (l, 0, n)),
            pl.BlockSpec((None, 1, tn), lambda l, n: (l, 0, n)),
        ],
        out_specs=pl.BlockSpec((None, MOD_ROWS, tn), lambda l, n: (l, 0, n)),
        out_shape=jax.ShapeDtypeStruct((depth, MOD_ROWS, width), F32),
        compiler_params=_params(("parallel", "parallel"), 40),
    )(cvec, w_ada, b_ada.reshape(depth, 1, width))


class _Rows:
    def __init__(self, m_ctx, dec_batch, dec_seq, d):
        self.m_ctx, self.dec_batch, self.dec_seq, self.d = m_ctx, dec_batch, dec_seq, d
        self.m_lat = dec_batch * dec_seq
        self.m = m_ctx + self.m_lat

    def group(self, i, tm):
        n_ctx = self.m_ctx // tm
        per_batch = self.dec_seq // tm
        return jnp.where(i < n_ctx, 0, 1 + (i - n_ctx) // per_batch)

    def mod_spec(self, layer, piece, tm, lag=0):
        return pl.BlockSpec((None, None, None, 1, self.d),
                            lambda i, *_: (layer, piece, self.group(jnp.maximum(i - lag, 0), tm), 0, 0))

    def gain_spec(self, layer):
        return pl.BlockSpec((None, 1, self.d), lambda i, *_: (layer, 0, 0))

    def split_specs(self, tm, width):
        n_ctx = self.m_ctx // tm
        ctx = pl.BlockSpec((tm, width), lambda i, *_: (jnp.minimum(i, n_ctx - 1), 0))
        lat = pl.BlockSpec((tm, width), lambda i, *_: (jnp.maximum(i - n_ctx, 0), 0))
        return ctx, lat

    def pick(self, tm, ctx_ref, lat_ref):
        return jnp.where(pl.program_id(0) < self.m_ctx // tm, ctx_ref[...], lat_ref[...])


def _prologue_kernel(xp_ref, xs_ref, g_ref, sc_ref, sh_ref, x_ref, h_ref, *, rows, tm):
    x = rows.pick(tm, xp_ref, xs_ref)
    x_ref[...] = x
    h = _rms(x, g_ref[...]) * (1.0 + sc_ref[...]) + sh_ref[...]
    h_ref[...] = h.astype(h_ref.dtype)


def _prologue(rows, x_ctx, x_lat, g_pre, mods, layer):
    tm = 512
    row_spec = pl.BlockSpec((tm, rows.d), lambda i: (i, 0))
    return pl.pallas_call(
        functools.partial(_prologue_kernel, rows=rows, tm=tm),
        name="prologue",
        grid=(rows.m // tm,),
        in_specs=[
            *rows.split_specs(tm, rows.d),
            rows.gain_spec(layer),
            rows.mod_spec(layer, MOD_SCALE1, tm),
            rows.mod_spec(layer, MOD_SHIFT1, tm),
        ],
        out_specs=[row_spec, row_spec],
        out_shape=[jax.ShapeDtypeStruct((rows.m, rows.d), F32),
                   jax.ShapeDtypeStruct((rows.m, rows.d), BF16)],
        compiler_params=_params(("arbitrary",), 40),
    )(x_ctx, x_lat, g_pre, mods, mods)


def _inproj_kernel(a_ref, w_ref, o_ref, wb_ref):
    @pl.when(pl.program_id(1) == 0)
    def _():
        wb_ref[...] = w_ref[...].astype(BF16)

    o_ref[...] = _dot(a_ref[...], wb_ref[...]).astype(o_ref.dtype)


def _project(h, w, layer, width, tn, out_dtype):
    m, k = h.shape
    tm = 1024
    assert width % tn == 0
    return pl.pallas_call(
        _inproj_kernel,
        name="in_proj",
        grid=(width // tn, m // tm),
        in_specs=[
            pl.BlockSpec((tm, k), lambda n, i: (i, 0)),
            pl.BlockSpec((None, k, tn), lambda n, i: (layer, 0, n)),
        ],
        out_specs=pl.BlockSpec((tm, tn), lambda n, i: (i, n)),
        out_shape=jax.ShapeDtypeStruct((m, width), out_dtype),
        scratch_shapes=[pltpu.VMEM((k, tn), BF16)],
        compiler_params=_params(("arbitrary", "arbitrary"), 58),
    )(h, w)


CTX_BATCHES_PER_STEP = 2


def _ctx_attn_kernel(q_ref, k_ref, v_ref, o_ref, ko_ref, vo_ref, *, n_heads, scale):
    ko_ref[...] = k_ref[...].reshape(ko_ref.shape)
    vo_ref[...] = v_ref[...].reshape(vo_ref.shape)
    for h in range(n_heads):
        cols = slice(h * HEAD_DIM, (h + 1) * HEAD_DIM)
        q = (q_ref[:, cols] * scale).astype(BF16)
        k = k_ref[:, cols].astype(BF16)
        v = v_ref[:, cols].astype(BF16)
        s = _dot_nt(q, k)
        p = jnp.exp(s - jnp.max(s, axis=-1, keepdims=True))
        denom = jnp.sum(p, axis=-1, keepdims=True)
        o_ref[:, cols] = (_dot(p.astype(BF16), v) / denom).astype(o_ref.dtype)


def _ctx_mixers(proj, new_k, new_v, layer, depth, batch, seq, w_a, col_f, w_f):
    n_heads = w_a // HEAD_DIM
    attend = functools.partial(_ctx_attn_kernel, n_heads=n_heads, scale=1.0 / math.sqrt(HEAD_DIM))
    fourier, tables, table_specs = _fourier_operands(seq, w_f)
    f_block = col_f // w_f
    assert f_block * w_f == col_f
    bt = CTX_BATCHES_PER_STEP
    assert batch % bt == 0
    cache_spec = pl.BlockSpec((bt, None, seq, n_heads, HEAD_DIM), lambda b: (b, layer, 0, 0, 0))
    cache_shape = jax.ShapeDtypeStruct((batch, depth, seq, n_heads, HEAD_DIM), F32)
    in_specs = [pl.BlockSpec((bt * seq, w_a), lambda b, j=j: (b, j)) for j in range(3)]
    in_specs += [pl.BlockSpec((bt * seq, w_f), lambda b: (b, f_block))] + table_specs
    in_specs += [pl.BlockSpec(memory_space=pl.ANY)] * 2
    n_in = len(in_specs)

    def body(q_ref, k_ref, v_ref, f_ref, wc_ref, cn_ref, sn_ref, k_all_ref, v_all_ref,
             o_ref, of_ref, ko_ref, vo_ref):
        del k_all_ref, v_all_ref
        for t in range(bt):
            rws = pl.ds(t * seq, seq)
            attend(q_ref.at[rws], k_ref.at[rws], v_ref.at[rws], o_ref.at[rws],
                   ko_ref.at[t], vo_ref.at[t])
            fourier(f_ref.at[rws], wc_ref, cn_ref, sn_ref, of_ref.at[rws])

    return pl.pallas_call(
        body,
        name="ctx_mixers",
        grid=(batch // bt,),
        in_specs=in_specs,
        out_specs=[pl.BlockSpec((bt * seq, w_a), lambda b: (b, 0)),
                   pl.BlockSpec((bt * seq, w_f), lambda b: (b, 0)), cache_spec, cache_spec],
        out_shape=[jax.ShapeDtypeStruct((batch * seq, w_a), BF16),
                   jax.ShapeDtypeStruct((batch * seq, w_f), BF16), cache_shape, cache_shape],
        input_output_aliases={n_in - 2: 2, n_in - 1: 3},
        compiler_params=_params(("arbitrary",), 32),
    )(proj, proj, proj, proj, *tables, new_k, new_v)


def _window_start(r, rows):
    win_r = min(MAX_WIN_R, rows)
    return min(max(r - win_r // 2, 0), rows - win_r)


QUERY_GROUP_ROWS = 4


def _query_groups(rows):
    win_r = min(MAX_WIN_R, rows)
    span = min(rows, win_r + QUERY_GROUP_ROWS)
    assert rows % QUERY_GROUP_ROWS == 0
    groups = []
    for r0 in range(0, rows, QUERY_GROUP_ROWS):
        k0 = min(_window_start(r0, rows), rows - span)
        for r in range(r0, r0 + QUERY_GROUP_ROWS):
            assert k0 <= _window_start(r, rows) and _window_start(r, rows) + win_r <= k0 + span
        groups.append((r0, k0))
    return groups, span


def _bias_kernel(rpb_ref, o_ref, *, rows):
    win_r = min(MAX_WIN_R, rows)
    lanes = rpb_ref.shape[1]
    q = lax.broadcasted_iota(jnp.int32, (GRID_W, lanes), 0)
    kc = lax.broadcasted_iota(jnp.int32, (GRID_W, lanes), 1)
    c_start = jnp.clip(q - WIN_C // 2, 0, GRID_W - WIN_C)
    valid = (kc >= c_start) & (kc < c_start + WIN_C)
    tables = []
    for dr in range(2 * MAX_WIN_R - 1):
        row = jnp.broadcast_to(rpb_ref[dr:dr + 1, :], (GRID_W, lanes))
        t = pltpu.roll(row, lanes - (WIN_C - 1), 1, stride=1, stride_axis=0)
        tables.append(jnp.where(valid, t, MASKED_SCORE)[:, :GRID_W])
    masked = jnp.full((GRID_W, GRID_W), MASKED_SCORE, F32)
    groups, span = _query_groups(rows)
    for g, (r0, k0) in enumerate(groups):
        for j in range(QUERY_GROUP_ROWS):
            start = _window_start(r0 + j, rows)
            for i in range(span):
                in_window = start <= k0 + i < start + win_r
                dr = k0 + i - (r0 + j) + (MAX_WIN_R - 1)
                o_ref[g, j * GRID_W:(j + 1) * GRID_W, i * GRID_W:(i + 1) * GRID_W] = (
                    tables[dr] if in_window else masked)


def _window_bias(rpb, rows):
    depth, n_heads, n_dr, n_dc = rpb.shape
    groups, span = _query_groups(rows)
    block = (len(groups), QUERY_GROUP_ROWS * GRID_W, span * GRID_W)
    dr_pad = -(-n_dr // SUBLANES) * SUBLANES
    rpb_rows = jnp.pad(rpb.reshape(depth * n_heads, n_dr, n_dc),
                       ((0, 0), (0, dr_pad - n_dr), (0, LANES - n_dc)))
    return pl.pallas_call(
        functools.partial(_bias_kernel, rows=rows),
        name="window_bias",
        grid=(depth * n_heads,),
        in_specs=[pl.BlockSpec((None, dr_pad, LANES), lambda i: (i, 0, 0))],
        out_specs=pl.BlockSpec((None,) + block, lambda i: (i, 0, 0, 0)),
        out_shape=jax.ShapeDtypeStruct((depth * n_heads,) + block, F32),
        compiler_params=_params(("parallel",), 32),
    )(rpb_rows)


def _lat_attn_kernel(q_ref, k_ref, v_ref, ck_ref, cv_ref, bias_ref, o_ref, *, rows, scale):
    groups, span = _query_groups(rows)
    ck = ck_ref[...].astype(BF16)
    cv = cv_ref[...].astype(BF16)
    for g, (r0, k0) in enumerate(groups):
        q_rows = slice(r0 * GRID_W, (r0 + QUERY_GROUP_ROWS) * GRID_W)
        k_rows = slice(k0 * GRID_W, (k0 + span) * GRID_W)
        q = (q_ref[q_rows, :] * scale).astype(BF16)
        kw = k_ref[k_rows, :].astype(BF16)
        vw = v_ref[k_rows, :].astype(BF16)
        s_lat = _dot_nt(q, kw) + bias_ref[g]
        s_ctx = _dot_nt(q, ck)
        top = jnp.maximum(jnp.max(s_lat, axis=-1, keepdims=True),
                          jnp.max(s_ctx, axis=-1, keepdims=True))
        p_lat = jnp.exp(s_lat - top)
        p_ctx = jnp.exp(s_ctx - top)
        denom = jnp.sum(p_lat, axis=-1, keepdims=True) + jnp.sum(p_ctx, axis=-1, keepdims=True)
        o = _dot(p_lat.astype(BF16), vw) + _dot(p_ctx.astype(BF16), cv)
        o_ref[q_rows, :] = (o / denom).astype(o_ref.dtype)


def _lat_attention(proj, cache_k, cache_v, bias, layer, m_ctx, dec_batch, dec_seq, w_a):
    n_heads = w_a // HEAD_DIM
    rows = dec_seq // GRID_W
    past = cache_k.shape[2]
    first = m_ctx // dec_seq
    assert first * dec_seq == m_ctx
    kern = functools.partial(_lat_attn_kernel, rows=rows, scale=1.0 / math.sqrt(HEAD_DIM))
    qkv_specs = [pl.BlockSpec((dec_seq, HEAD_DIM), lambda h, b, j=j: (first + b, j * n_heads + h))
                 for j in range(3)]
    cache_spec = pl.BlockSpec((None, None, past, HEAD_DIM), lambda h, b: (b, layer, 0, h))
    return pl.pallas_call(
        kern,
        name="lat_attn",
        grid=(n_heads, dec_batch),
        in_specs=qkv_specs + [
            cache_spec, cache_spec,
            pl.BlockSpec((None,) + bias.shape[1:], lambda h, b: (layer * n_heads + h, 0, 0, 0)),
        ],
        out_specs=pl.BlockSpec((dec_seq, HEAD_DIM), lambda h, b: (b, h)),
        out_shape=jax.ShapeDtypeStruct((dec_batch * dec_seq, w_a), BF16),
        compiler_params=_params(("parallel", "parallel"), 32),
    )(proj, proj, proj, cache_k, cache_v, bias)


def _dft_tables(n):
    idx = np.arange(n, dtype=np.int64)
    ang = 2.0 * np.pi * ((idx[:, None] * idx[None, :]) % n) / n
    return np.cos(ang), np.sin(ang)


def _fourier_kernel(f_ref, wc_ref, cn_ref, sn_ref, o_ref, *, n_groups, scale):
    wc = wc_ref[...].astype(BF16)
    pc, ps = [], []
    for g in range(n_groups):
        p = _dot(f_ref[:, g * GROUP_DIM:(g + 1) * GROUP_DIM].astype(BF16), wc)
        pc.append(p[:, :GROUP_DIM])
        ps.append(p[:, GROUP_DIM:])
    y = (_dot(cn_ref[...].astype(BF16), jnp.concatenate(pc, axis=1).astype(BF16))
         - _dot(sn_ref[...].astype(BF16), jnp.concatenate(ps, axis=1).astype(BF16)))
    o_ref[...] = (y * scale).astype(o_ref.dtype)


def _fourier_operands(n_pos, w_f):
    cc, sc = _dft_tables(GROUP_DIM)
    cn, sn = _dft_tables(n_pos)
    tables = tuple(jnp.asarray(t, F32) for t in (np.concatenate([cc, sc], axis=1), cn, sn))
    specs = [pl.BlockSpec(t.shape, lambda b: (0, 0)) for t in tables]
    kern = functools.partial(_fourier_kernel, n_groups=w_f // GROUP_DIM,
                             scale=1.0 / math.sqrt(n_pos * GROUP_DIM))
    return kern, tables, specs


def _fourier(proj, col_start, w_f, n_batch, n_pos, first_block):
    col_block = col_start // w_f
    assert col_block * w_f == col_start
    kern, tables, table_specs = _fourier_operands(n_pos, w_f)
    return pl.pallas_call(
        kern,
        name="fourier",
        grid=(n_batch,),
        in_specs=[pl.BlockSpec((n_pos, w_f), lambda b: (first_block + b, col_block))] + table_specs,
        out_specs=pl.BlockSpec((n_pos, w_f), lambda b: (b, 0)),
        out_shape=jax.ShapeDtypeStruct((n_batch * n_pos, w_f), BF16),
        compiler_params=_params(("parallel",), 48),
    )(proj, *tables)


def _gelu(x):
    return 0.5 * x * (1.0 + lax.erf(x * math.sqrt(0.5)))


def _sgu_kernel(u_ref, v_ref, g_ref, w_ref, b_ref, o_ref, *, n_groups, n_chunks):
    for g in range(n_groups):
        cols = slice(g * GROUP_DIM, (g + 1) * GROUP_DIM)
        v = _rms(_gelu(v_ref[:, cols]), g_ref[:, cols]).astype(BF16)
        v_wide = jnp.concatenate([v[c * CHUNK:(c + 1) * CHUNK, :] for c in range(n_chunks)], axis=1)
        s = _dot(w_ref[g], v_wide) + b_ref[:, g:g + 1]
        for c in range(n_chunks):
            rws = slice(c * CHUNK, (c + 1) * CHUNK)
            o_ref[rws, cols] = (_gelu(u_ref[rws, cols])
                                * s[:, c * GROUP_DIM:(c + 1) * GROUP_DIM]).astype(o_ref.dtype)


def _spatial_gating(proj, col_start, w_c, g_sgu, w_sp, b_sp_t, layer):
    m = proj.shape[0]
    n_groups = w_c // GROUP_DIM
    n_chunks = 8
    tm = n_chunks * CHUNK
    col_block = col_start // w_c
    assert col_block * w_c == col_start
    kern = functools.partial(_sgu_kernel, n_groups=n_groups, n_chunks=n_chunks)
    return pl.pallas_call(
        kern,
        name="spatial_gate",
        grid=(m // tm,),
        in_specs=[
            pl.BlockSpec((tm, w_c), lambda i: (i, col_block)),
            pl.BlockSpec((tm, w_c), lambda i: (i, col_block + 1)),
            pl.BlockSpec((None, 1, w_c), lambda i: (layer, 0, 0)),
            pl.BlockSpec((None, n_groups, CHUNK, CHUNK), lambda i: (layer, 0, 0, 0)),
            pl.BlockSpec((None, CHUNK, n_groups), lambda i: (layer, 0, 0)),
        ],
        out_specs=pl.BlockSpec((tm, w_c), lambda i: (i, 0)),
        out_shape=jax.ShapeDtypeStruct((m, w_c), BF16),
        compiler_params=_params(("parallel",), 32),
    )(proj, proj, g_sgu, w_sp, b_sp_t)


def _merge_kernel(h_ref, oa_ctx_ref, oa_lat_ref, of_ctx_ref, of_lat_ref, oc_ref,
                  wga_ref, wgf_ref, wgc_ref, wa_ref, wf_ref, wc_ref, w1_ref, w2_ref,
                  o_ref, w1o_ref, w2o_ref, *, rows, tm):
    w1o_ref[...] = w1_ref[...].astype(BF16)
    w2o_ref[...] = w2_ref[...].astype(BF16)
    h = h_ref[...]
    o_a = rows.pick(tm, oa_ctx_ref, oa_lat_ref)
    o_f = rows.pick(tm, of_ctx_ref, of_lat_ref)
    acc = jax.nn.sigmoid(_dot(h, wga_ref[...])) * _dot(o_a, wa_ref[...])
    acc += jax.nn.sigmoid(_dot(h, wgf_ref[...])) * _dot(o_f, wf_ref[...])
    acc += jax.nn.sigmoid(_dot(h, wgc_ref[...])) * _dot(oc_ref[...], wc_ref[...])
    o_ref[...] = acc.astype(o_ref.dtype)


def _merge(rows, h, o_a_ctx, o_a_lat, o_f_ctx, o_f_lat, o_c, w_gates, w_br_a, w_br_f, w_br_c,
           w_mlp1, w_mlp2, layer):
    d = rows.d
    d_ff = w_mlp1.shape[2]
    tm, tn = 1024, 256
    nb = d // tn
    n_steps = (rows.m // tm) * (nb - 1)
    n_slabs = 1 << (n_steps.bit_length() - 1)
    slab1, slab2 = d // n_slabs, d_ff // n_slabs
    assert slab1 * n_slabs == d and slab2 * n_slabs == d_ff and slab1 % 16 == 0

    def slab(i, n):
        return jnp.minimum(i * (nb - 1) + jnp.minimum(n, nb - 2), n_slabs - 1)

    full = lambda width: pl.BlockSpec((tm, width), lambda i, n: (i, 0))
    gate_w = lambda j: pl.BlockSpec((d, tn), lambda i, n: (0, j * nb + n))
    br_w = lambda w: pl.BlockSpec((None, w.shape[1], tn), lambda i, n: (layer, 0, n))
    return pl.pallas_call(
        functools.partial(_merge_kernel, rows=rows, tm=tm),
        name="gate_merge",
        grid=(rows.m // tm, nb),
        in_specs=[full(d),
                  *rows.split_specs(tm, o_a_ctx.shape[1]),
                  *rows.split_specs(tm, o_f_ctx.shape[1]),
                  full(o_c.shape[1]),
                  gate_w(0), gate_w(1), gate_w(2),
                  br_w(w_br_a), br_w(w_br_f), br_w(w_br_c),
                  pl.BlockSpec((None, slab1, d_ff), lambda i, n: (layer, slab(i, n), 0)),
                  pl.BlockSpec((None, slab2, d), lambda i, n: (layer, slab(i, n), 0))],
        out_specs=[pl.BlockSpec((tm, tn), lambda i, n: (i, n)),
                   pl.BlockSpec((slab1, d_ff), lambda i, n: (slab(i, n), 0)),
                   pl.BlockSpec((slab2, d), lambda i, n: (slab(i, n), 0))],
        out_shape=[jax.ShapeDtypeStruct((rows.m, d), BF16),
                   jax.ShapeDtypeStruct((d, d_ff), BF16),
                   jax.ShapeDtypeStruct((d_ff, d), BF16)],
        compiler_params=_params(("arbitrary", "arbitrary"), 56),
    )(h, o_a_ctx, o_a_lat, o_f_ctx, o_f_lat, o_c, w_gates, w_gates, w_gates,
      w_br_a, w_br_f, w_br_c, w_mlp1, w_mlp2)


SUB_ROWS = 256


OVERLAP_PIECES = 4


def _unit_rms(y):
    return y * lax.rsqrt(jnp.mean(y * y, axis=-1, keepdims=True) + RMS_EPS)


def _outproj_kernel(mg_ref, w_ref, x_ref, gpost_ref, gt_ref, gpre_ref, sc_ref, sh_ref,
                    xo_ref, ho_ref, y_even_ref, y_odd_ref, *, n_tiles):
    s = pl.program_id(0)
    y_refs = (y_even_ref, y_odd_ref)

    def multiply(parity):
        y_refs[parity][...] = _dot(mg_ref[...], w_ref[...])

    def finish(parity, rws=slice(None)):
        post = gpost_ref[...] * gt_ref[...]
        pre = gpre_ref[...] * (1.0 + sc_ref[...])
        x1 = x_ref[rws, :] + _unit_rms(y_refs[parity][rws, :]) * post
        xo_ref[rws, :] = x1
        ho_ref[rws, :] = (_unit_rms(x1) * pre + sh_ref[...]).astype(ho_ref.dtype)
        return x1

    def multiply_and_finish(parity):
        tm, d = mg_ref.shape
        rp, cw = tm // OVERLAP_PIECES, d // OVERLAP_PIECES
        lhs = mg_ref[...]
        for j in range(OVERLAP_PIECES):
            cols = slice(j * cw, (j + 1) * cw)
            y_refs[parity][:, cols] = _dot(lhs, w_ref[:, cols])
            x1 = finish(1 - parity, slice(j * rp, (j + 1) * rp))
            lhs = mg_ref[...] + jnp.minimum(jnp.abs(x1[:1, :1]), 0.0).astype(BF16)

    @pl.when(s == 0)
    def _():
        multiply(0)

    for parity in (0, 1):
        @pl.when((s > 0) & (s < n_tiles) & (s % 2 == parity))
        def _():
            multiply_and_finish(parity)

    @pl.when(s == n_tiles)
    def _():
        finish((n_tiles - 1) % 2)


def _out_projection(rows, merged, w_out, x, g_post_mix, g_pre_mlp, mods, layer):
    tm = 512
    d = rows.d
    n_tiles = rows.m // tm
    ahead_spec = pl.BlockSpec((tm, d), lambda s: (jnp.minimum(s, n_tiles - 1), 0))
    lag_spec = pl.BlockSpec((tm, d), lambda s: (jnp.maximum(s - 1, 0), 0))
    return pl.pallas_call(
        functools.partial(_outproj_kernel, n_tiles=n_tiles),
        name="out_proj",
        grid=(n_tiles + 1,),
        in_specs=[
            ahead_spec,
            pl.BlockSpec((d, d), lambda s: (0, 0), pipeline_mode=pl.Buffered(1)),
            lag_spec,
            rows.gain_spec(layer),
            rows.mod_spec(layer, MOD_GATE1, tm, lag=1),
            rows.gain_spec(layer),
            rows.mod_spec(layer, MOD_SCALE2, tm, lag=1),
            rows.mod_spec(layer, MOD_SHIFT2, tm, lag=1),
        ],
        out_specs=[lag_spec, lag_spec],
        out_shape=[jax.ShapeDtypeStruct((rows.m, d), F32), jax.ShapeDtypeStruct((rows.m, d), BF16)],
        scratch_shapes=[pltpu.VMEM((tm, d), F32), pltpu.VMEM((tm, d), F32)],
        compiler_params=_params(("arbitrary",), 52),
    )(merged, w_out, x, g_post_mix, mods, g_pre_mlp, mods, mods)


def _mlp_kernel(*refs, n_ff_steps, has_next, rows, tm):
    if has_next:
        (h_ref, w1_ref, w2_ref, x_ref, gpost_ref, gt_ref, gpre_ref, sc_ref, sh_ref,
         wg_ref, wo_ref, xo_ref, ho_ref, wgo_ref, woo_ref, acc_ref) = refs
        wgo_ref[...] = wg_ref[...].astype(BF16)
        woo_ref[...] = wo_ref[...].astype(BF16)
    else:
        h_ref, w1_ref, w2_ref, x_ref, gpost_ref, gt_ref, yp_ref, ys_ref, acc_ref = refs
    step = pl.program_id(1)
    last = n_ff_steps - 1
    assert last >= 2

    def ff_slice(rws):
        hid = jnp.square(jnp.maximum(_dot(h_ref[rws, :], w1_ref[...]), 0.0)).astype(BF16)
        return _dot(hid, w2_ref[...])

    whole = slice(None)

    @pl.when(step == 0)
    def _():
        acc_ref[...] = ff_slice(whole)

    @pl.when((step > 0) & (step < last))
    def _():
        acc_ref[...] += ff_slice(whole)

    @pl.when(step == last)
    def _():
        post = gpost_ref[...] * gt_ref[...]
        if has_next:
            pre = gpre_ref[...] * (1.0 + sc_ref[...])
            shift = sh_ref[...]
        for c in range(tm // SUB_ROWS):
            rws = slice(c * SUB_ROWS, (c + 1) * SUB_ROWS)
            x2 = x_ref[rws, :] + _unit_rms(acc_ref[rws, :] + ff_slice(rws)) * post
            if has_next:
                xo_ref[rws, :] = x2
                ho_ref[rws, :] = (_unit_rms(x2) * pre + shift).astype(ho_ref.dtype)
            else:
                acc_ref[rws, :] = x2
        if not has_next:
            is_ctx = pl.program_id(0) < rows.m_ctx // tm

            @pl.when(is_ctx)
            def _():
                yp_ref[...] = acc_ref[...]

            @pl.when(jnp.logical_not(is_ctx))
            def _():
                ys_ref[...] = acc_ref[...]


def _mlp(rows, h2, w1, w2, x, g_post_mlp, g_pre_next, mods, w_in, col_gates, w_out, layer, has_next):
    tm, tf = 512, 1024
    d = rows.d
    d_ff = w1.shape[1]
    n_ff_steps = d_ff // tf
    row_spec = pl.BlockSpec((tm, d), lambda i, f: (i, 0))
    in_specs = [
        row_spec,
        pl.BlockSpec((d, tf), lambda i, f: (0, f)),
        pl.BlockSpec((tf, d), lambda i, f: (f, 0)),
        row_spec,
        rows.gain_spec(layer),
        rows.mod_spec(layer, MOD_GATE2, tm),
    ]
    args = [h2, w1, w2, x, g_post_mlp, mods]
    if has_next:
        in_specs += [rows.gain_spec(layer + 1),
                     rows.mod_spec(layer + 1, MOD_SCALE1, tm),
                     rows.mod_spec(layer + 1, MOD_SHIFT1, tm)]
        n_steps = (rows.m // tm) * (n_ff_steps - 1)
        n_slabs = 1 << (n_steps.bit_length() - 1)
        gate_width = w_in.shape[2] - col_gates
        piece = math.gcd(col_gates, gate_width)
        n_pieces = gate_width // piece
        g_rows, o_rows = d * n_pieces // n_slabs, d // n_slabs
        assert g_rows % 16 == 0 and o_rows % 16 == 0 and g_rows * n_slabs == d * n_pieces

        def slab(i, f):
            return jnp.minimum(i * (n_ff_steps - 1) + jnp.minimum(f, n_ff_steps - 2), n_slabs - 1)

        in_specs += [
            pl.BlockSpec((None, g_rows, piece),
                         lambda i, f: (layer + 1, slab(i, f) // n_pieces,
                                       col_gates // piece + slab(i, f) % n_pieces)),
            pl.BlockSpec((None, o_rows, d), lambda i, f: (layer + 1, slab(i, f), 0)),
        ]
        args += [g_pre_next, mods, mods, w_in, w_out]
        out_specs = [row_spec, row_spec,
                     pl.BlockSpec((g_rows, piece),
                                  lambda i, f: (slab(i, f) // n_pieces, slab(i, f) % n_pieces)),
                     pl.BlockSpec((o_rows, d), lambda i, f: (slab(i, f), 0))]
        out_shape = [jax.ShapeDtypeStruct((rows.m, d), F32),
                     jax.ShapeDtypeStruct((rows.m, d), BF16),
                     jax.ShapeDtypeStruct((d, gate_width), BF16),
                     jax.ShapeDtypeStruct((d, d), BF16)]
    else:
        out_specs = list(rows.split_specs(tm, d))
        out_shape = [jax.ShapeDtypeStruct((rows.m_ctx, d), F32),
                     jax.ShapeDtypeStruct((rows.m_lat, d), F32)]
    return pl.pallas_call(
        functools.partial(_mlp_kernel, n_ff_steps=n_ff_steps, has_next=has_next, rows=rows, tm=tm),
        name="mlp",
        grid=(rows.m // tm, n_ff_steps),
        in_specs=in_specs,
        out_specs=out_specs,
        out_shape=out_shape,
        scratch_shapes=[pltpu.VMEM((tm, d), F32)],
        compiler_params=_params(("arbitrary", "arbitrary"), 56),
    )(*args)


def kernel(x_prompt, x_sample, cache_k, cache_v, c, c_ctx, w_ada, b_ada, g_pre_mix, g_post_mix,
           g_pre_mlp, g_post_mlp, w_in, rpb, g_sgu, w_spatial, b_spatial, w_br_a, w_br_f, w_br_c,
           w_out, w_mlp1, w_mlp2):
    batch, seq, d = x_prompt.shape
    dec_batch, dec_seq, _ = x_sample.shape
    depth = w_in.shape[0]
    past = cache_k.shape[2]
    w_a = w_br_a.shape[1]
    w_f = w_br_f.shape[1]
    w_c = w_br_c.shape[1]
    n_heads = w_a // HEAD_DIM
    m_ctx = batch * seq
    rows = _Rows(m_ctx, dec_batch, dec_seq, d)
    assert dec_seq % GRID_W == 0 and m_ctx % dec_seq == 0 and MOD_ROWS >= 1 + dec_batch

    cvec = jnp.zeros((MOD_ROWS, d), F32).at[0].set(c_ctx).at[1:1 + dec_batch].set(c)
    mods = _ada(cvec, w_ada, b_ada)
    mods = mods[:, :1 + dec_batch].reshape(depth, 1 + dec_batch, N_MOD, d)
    mods = mods.transpose(0, 2, 1, 3)[:, :, :, None, :]

    col_f = 3 * w_a
    col_uv = col_f + w_f
    col_gates = col_uv + 2 * w_c

    w_gates_b = w_in[0, :, col_gates:].astype(BF16)
    w_out_b = w_out[0].astype(BF16)
    w_br_a_b, w_br_f_b, w_br_c_b = w_br_a.astype(BF16), w_br_f.astype(BF16), w_br_c.astype(BF16)
    w_sp_b = w_spatial.astype(BF16)
    b_sp_t = b_spatial.transpose(0, 2, 1)
    g_sgu_row = g_sgu.reshape(depth, 1, w_c)
    gains = [g.reshape(depth, 1, d) for g in (g_pre_mix, g_post_mix, g_pre_mlp, g_post_mlp)]
    g_pre_mix_r, g_post_mix_r, g_pre_mlp_r, g_post_mlp_r = gains
    cache_k_r = cache_k.reshape(dec_batch, depth, past, w_a)
    cache_v_r = cache_v.reshape(dec_batch, depth, past, w_a)
    bias = _window_bias(rpb, dec_seq // GRID_W)

    x, h = _prologue(rows, x_prompt.reshape(m_ctx, d), x_sample.reshape(rows.m_lat, d),
                     g_pre_mix_r, mods, 0)
    new_k = jnp.zeros((batch, depth, seq, n_heads, HEAD_DIM), F32)
    new_v = jnp.zeros((batch, depth, seq, n_heads, HEAD_DIM), F32)
    for l in range(depth):
        proj = _project(h, w_in, l, col_gates, 1536, F32)
        o_a_ctx, o_f_ctx, new_k, new_v = _ctx_mixers(proj, new_k, new_v, l, depth, batch, seq, w_a,
                                                     col_f, w_f)
        o_a_lat = _lat_attention(proj, cache_k_r, cache_v_r, bias, l, m_ctx, dec_batch, dec_seq, w_a)
        o_f_lat = _fourier(proj, col_f, w_f, dec_batch, dec_seq, m_ctx // dec_seq)
        o_c = _spatial_gating(proj, col_uv, w_c, g_sgu_row, w_sp_b, b_sp_t, l)
        merged, w1_b, w2_b = _merge(rows, h, o_a_ctx, o_a_lat, o_f_ctx, o_f_lat, o_c, w_gates_b,
                                    w_br_a_b, w_br_f_b, w_br_c_b, w_mlp1, w_mlp2, l)
        x, h2 = _out_projection(rows, merged, w_out_b, x, g_post_mix_r, g_pre_mlp_r, mods, l)
        outs = _mlp(rows, h2, w1_b, w2_b, x, g_post_mlp_r, g_pre_mix_r, mods, w_in, col_gates,
                    w_out, l, l + 1 < depth)
        if l + 1 < depth:
            x, h, w_gates_b, w_out_b = outs

    y_p = outs[0].reshape(batch, seq, d)
    y_s = outs[1].reshape(dec_batch, dec_seq, d)
    cache_shape = (batch, depth, seq, n_heads, HEAD_DIM)
    return (y_p, y_s, new_k.reshape(cache_shape), new_v.reshape(cache_shape))
```
